```python
import math
import jax, jax.numpy as jnp
from jax import lax
import numpy as np

D_MODEL = 4096
BATCH = 4
SEQ = 4096
DEPTH = 1

N_META = 16
MIX_WIDTH = D_MODEL
ATTN_WIDTH = MIX_WIDTH // 2
FOURIER_WIDTH = MIX_WIDTH - ATTN_WIDTH
DIFF_HEAD_DIM = 64
DIFF_V_DIM = 2 * DIFF_HEAD_DIM
N_DIFF_HEADS = ATTN_WIDTH // DIFF_V_DIM
QK_WIDTH = N_DIFF_HEADS * 2 * DIFF_HEAD_DIM
N_FOURIER_GROUPS = 4
FOURIER_GROUP_DIM = FOURIER_WIDTH // N_FOURIER_GROUPS
IN_WIDTH = 2 * QK_WIDTH + ATTN_WIDTH + FOURIER_WIDTH
D_FF = 256 * ((int(8 * D_MODEL / 3) + 255) // 256)
CONV_WIDTH = 3
Q_BLOCK = 128
NORM_EPS = 1e-6
SUBLN_EPS = 1e-5

kernel_name = "hymba_diffattn_fnet_convglu_encoder"


def rms_norm(x, w, eps=NORM_EPS):
    xf = x.astype(jnp.float32)
    y = xf * lax.rsqrt(jnp.mean(xf * xf, axis=-1, keepdims=True) + eps)
    return (y * w.astype(jnp.float32)).astype(x.dtype)


def alibi_slopes(n_heads):
    return jnp.exp2(-8.0 * jnp.arange(1, n_heads + 1, dtype=jnp.float32) / n_heads)


def diff_attention(q, k, v, lam):
    b, l, _ = q.shape
    q = q.reshape(b, l, N_DIFF_HEADS, 2, DIFF_HEAD_DIM).transpose(0, 2, 3, 1, 4)
    k = k.reshape(b, l, N_DIFF_HEADS, 2, DIFF_HEAD_DIM).transpose(0, 2, 3, 1, 4)
    v = v.reshape(b, l, N_DIFF_HEADS, DIFF_V_DIM).transpose(0, 2, 1, 3)
    slopes = alibi_slopes(N_DIFF_HEADS)
    k_pos = jnp.arange(l, dtype=jnp.float32)
    scale = 1.0 / math.sqrt(DIFF_HEAD_DIM)

    def attend(qb, q_pos):
        s = jnp.einsum('bhcqd,bhckd->bhcqk', qb, k).astype(jnp.float32) * scale
        bias = -slopes[:, None, None] * jnp.abs(q_pos[:, None] - k_pos[None, :])
        p = jax.nn.softmax(s + bias[None, :, None], axis=-1)
        a = p[:, :, 0] - lam * p[:, :, 1]
        return jnp.einsum('bhqk,bhkd->bhqd', a.astype(v.dtype), v)

    out_meta = attend(q[:, :, :, :N_META], k_pos[:N_META])
    n_blocks = (l - N_META) // Q_BLOCK
    q_real = q[:, :, :, N_META:].reshape(b, N_DIFF_HEADS, 2, n_blocks, Q_BLOCK, DIFF_HEAD_DIM)
    q_real = jnp.moveaxis(q_real, 3, 0)
    pos_real = k_pos[N_META:].reshape(n_blocks, Q_BLOCK)
    out_real = lax.map(lambda args: attend(args[0], args[1]), (q_real, pos_real))
    out_real = jnp.moveaxis(out_real, 0, 2).reshape(b, N_DIFF_HEADS, n_blocks * Q_BLOCK, DIFF_V_DIM)
    return jnp.concatenate([out_meta, out_real], axis=2)


def fourier_mix(f, norm_w):
    b, l, _ = f.shape
    fg = f.astype(jnp.float32).reshape(b, l, N_FOURIER_GROUPS, FOURIER_GROUP_DIM)
    fr = jnp.fft.fft2(fg, axes=(1, 3), norm="ortho").real
    fr = rms_norm(fr, norm_w.reshape(N_FOURIER_GROUPS, FOURIER_GROUP_DIM))
    return fr.reshape(b, l, FOURIER_WIDTH).astype(f.dtype)


def dwconv_centred(h, w, bias):
    hp = jnp.pad(h, ((0, 0), (1, 1), (0, 0)))
    return hp[:, :-2] * w[0] + hp[:, 1:-1] * w[1] + hp[:, 2:] * w[2] + bias


def setup_inputs(seed: int = 0) -> dict:
    key = jax.random.key(seed)
    ks = jax.random.split(key, 16)
    f32 = jnp.float32
    nrm = lambda k, shape, s: jax.random.normal(k, shape, f32) * s
    gain = lambda k, shape: 1.0 + 0.02 * jax.random.normal(k, shape, f32)
    return {
        "x": nrm(ks[0], (BATCH, SEQ, D_MODEL), 1.0),
        "meta_tokens": nrm(ks[1], (N_META, D_MODEL), 1.0),
        "norm1_w": gain(ks[2], (DEPTH, D_MODEL)),
        "w_in": nrm(ks[3], (DEPTH, D_MODEL, IN_WIDTH), D_MODEL ** -0.5),
        "lambda_q1": nrm(ks[4], (DEPTH, DIFF_HEAD_DIM), 0.1),
        "lambda_k1": nrm(ks[5], (DEPTH, DIFF_HEAD_DIM), 0.1),
        "lambda_q2": nrm(ks[6], (DEPTH, DIFF_HEAD_DIM), 0.1),
        "lambda_k2": nrm(ks[7], (DEPTH, DIFF_HEAD_DIM), 0.1),
        "diff_subln_w": gain(ks[8], (DEPTH, DIFF_V_DIM)),
        "fourier_norm_w": gain(ks[9], (DEPTH, FOURIER_WIDTH)),
        "w_out": nrm(ks[10], (DEPTH, MIX_WIDTH, D_MODEL), MIX_WIDTH ** -0.5),
        "norm2_w": gain(ks[11], (DEPTH, D_MODEL)),
        "w_up": nrm(ks[12], (DEPTH, D_MODEL, 2 * D_FF), D_MODEL ** -0.5),
        "conv_w": nrm(ks[13], (DEPTH, CONV_WIDTH, D_FF), CONV_WIDTH ** -0.5),
        "conv_b": nrm(ks[14], (DEPTH, D_FF), 0.01),
        "w_down": nrm(ks[15], (DEPTH, D_FF, D_MODEL), D_FF ** -0.5),
        "final_norm_w": gain(jax.random.fold_in(key, 99), (D_MODEL,)),
    }


def reference(x, meta_tokens, norm1_w, w_in, lambda_q1, lambda_k1, lambda_q2, lambda_k2,
              diff_subln_w, fourier_norm_w, w_out, norm2_w, w_up, conv_w, conv_b, w_down,
              final_norm_w):
    b = x.shape[0]
    meta = jnp.broadcast_to(meta_tokens[None].astype(x.dtype), (b, N_META, D_MODEL))
    h = jnp.concatenate([meta, x], axis=1)
    l = h.shape[1]
    for layer in range(DEPTH):
        lambda_init = 0.8 - 0.6 * math.exp(-0.3 * layer)
        hn = rms_norm(h, norm1_w[layer])
        y = hn @ w_in[layer]
        q = y[..., :QK_WIDTH]
        k = y[..., QK_WIDTH:2 * QK_WIDTH]
        v = y[..., 2 * QK_WIDTH:2 * QK_WIDTH + ATTN_WIDTH]
        f = y[..., 2 * QK_WIDTH + ATTN_WIDTH:]
        lam = (jnp.exp(jnp.sum(lambda_q1[layer].astype(jnp.float32) * lambda_k1[layer].astype(jnp.float32)))
               - jnp.exp(jnp.sum(lambda_q2[layer].astype(jnp.float32) * lambda_k2[layer].astype(jnp.float32)))
               + lambda_init)
        attn = diff_attention(q, k, v, lam)
        attn = rms_norm(attn, diff_subln_w[layer], SUBLN_EPS) * (1.0 - lambda_init)
        attn = attn.transpose(0, 2, 1, 3).reshape(b, l, ATTN_WIDTH)
        four = fourier_mix(f, fourier_norm_w[layer])
        h = h + jnp.concatenate([attn.astype(h.dtype), four], axis=-1) @ w_out[layer]
        hn = rms_norm(h, norm2_w[layer])
        u = hn @ w_up[layer]
        gate, val = u[..., :D_FF], u[..., D_FF:]
        gate = jax.nn.gelu(dwconv_centred(gate, conv_w[layer], conv_b[layer]), approximate=False)
        h = h + (gate * val) @ w_down[layer]
    return rms_norm(h[:, N_META:], final_norm_w)
```

```python
import functools
import math

import jax
import jax.numpy as jnp
from jax import lax
from jax.experimental import pallas as pl
from jax.experimental.pallas import tpu as pltpu

N_META = 16
N_HEADS = 16
HEAD_DIM = 64
V_DIM = 128
N_GROUPS = 4
GROUP_DIM = 512
CONV_WIDTH = 3
NORM_EPS = 1e-6
SUBLN_EPS = 1e-5
LAMBDA_INIT = 0.8 - 0.6 * math.exp(-0.3 * 0)

VMEM_LIMIT_BYTES = 56 * 1024 * 1024
BF16 = jnp.bfloat16
F32 = jnp.float32


def _params(*sem):
    return pltpu.CompilerParams(dimension_semantics=sem, vmem_limit_bytes=VMEM_LIMIT_BYTES)


def _norm_matmul_kernel(x_ref, nw_ref, w_ref, o_ref, hn_ref):
    @pl.when(pl.program_id(1) == 0)
    def _():
        x = x_ref[...]
        inv = lax.rsqrt(jnp.mean(x * x, axis=-1, keepdims=True) + NORM_EPS)
        hn_ref[...] = (x * inv * nw_ref[...]).astype(BF16)

    o_ref[...] = jnp.dot(hn_ref[...], w_ref[...], preferred_element_type=F32).astype(o_ref.dtype)


def _norm_matmul(x, nw, w, *, tm, tn, n_cols=None):
    m, d = x.shape
    n = w.shape[1] if n_cols is None else n_cols
    return pl.pallas_call(
        _norm_matmul_kernel,
        grid=(m // tm, n // tn),
        in_specs=[
            pl.BlockSpec((tm, d), lambda i, j: (i, 0)),
            pl.BlockSpec((1, d), lambda i, j: (0, 0)),
            pl.BlockSpec((d, tn), lambda i, j: (0, j)),
        ],
        out_specs=pl.BlockSpec((tm, tn), lambda i, j: (i, j)),
        out_shape=jax.ShapeDtypeStruct((m, n), BF16),
        scratch_shapes=[pltpu.VMEM((tm, d), BF16)],
        compiler_params=_params("parallel", "arbitrary"),
        name="norm_matmul",
    )(x, nw.reshape(1, d), w)


def _attn_kernel(lq1_ref, lk1_ref, lq2_ref, lk2_ref, subw_ref, q_ref, k_ref, v_ref,
                 km_ref, vm_ref, o_ref, *, tq, tk, q_pos0):
    h = pl.program_id(1)
    qi = pl.program_id(2)
    n_kv = k_ref.shape[0] // tk

    slope = jnp.exp2(jnp.full((1, 1), -8.0 / N_HEADS, F32) * (h + 1).astype(F32))

    lane = lax.broadcasted_iota(jnp.int32, (1, V_DIM), 1)
    q = q_ref[...].astype(F32) * (1.0 / math.sqrt(HEAD_DIM))
    qs = (jnp.where(lane < HEAD_DIM, q, 0.0).astype(BF16),
          jnp.where(lane >= HEAD_DIM, q, 0.0).astype(BF16))

    q_pos = (q_pos0 + qi * tq + lax.broadcasted_iota(jnp.int32, (tq, 1), 0)).astype(F32)

    def step(k_c, v_c, k_pos, state):
        bias = slope * jnp.abs(q_pos - k_pos)
        new_state = []
        for c in range(2):
            m, l, acc = state[c]
            s = lax.dot_general(qs[c], k_c, (((1,), (1,)), ((), ())),
                                preferred_element_type=F32) - bias
            m_new = jnp.maximum(m, jnp.max(s, axis=-1, keepdims=True))
            alpha = jnp.exp(m - m_new)
            p = jnp.exp(s - m_new)
            l = alpha * l + jnp.sum(p, axis=-1, keepdims=True)
            acc = alpha * acc + jnp.dot(p.astype(BF16), v_c, preferred_element_type=F32)
            new_state.append((m_new, l, acc))
        return tuple(new_state)

    init = tuple((jnp.full((tq, 1), -jnp.inf, F32), jnp.zeros((tq, 1), F32),
                  jnp.zeros((tq, V_DIM), F32)) for _ in range(2))
    meta_pos = lax.broadcasted_iota(jnp.int32, (1, N_META), 1).astype(F32)
    state = step(km_ref[...], vm_ref[...], meta_pos, init)

    def body(c, state):
        start = pl.multiple_of(c * tk, tk)
        k_pos = (N_META + c * tk + lax.broadcasted_iota(jnp.int32, (1, tk), 1)).astype(F32)
        return step(k_ref[pl.ds(start, tk), :], v_ref[pl.ds(start, tk), :], k_pos, state)

    state = lax.fori_loop(0, n_kv, body, state)

    lam = (jnp.exp(jnp.sum(lq1_ref[...] * lk1_ref[...], axis=-1, keepdims=True))
           - jnp.exp(jnp.sum(lq2_ref[...] * lk2_ref[...], axis=-1, keepdims=True))
           + LAMBDA_INIT)
    (_, l1, acc1), (_, l2, acc2) = state
    a = acc1 / l1 - lam * (acc2 / l2)
    inv = lax.rsqrt(jnp.mean(a * a, axis=-1, keepdims=True) + SUBLN_EPS)
    o_ref[...] = (a * inv * subw_ref[...] * (1.0 - LAMBDA_INIT)).astype(o_ref.dtype)


def _attention(lams, subw, yq, y, y_meta, *, tq, tk, q_pos0, q_batched):
    b, seq, _ = y.shape
    n_q = yq.shape[1]
    h_blk = N_HEADS
    q_map = (lambda bi, h, i: (bi, i, h)) if q_batched else (lambda bi, h, i: (0, i, h))
    vec = lambda n: pl.BlockSpec((1, n), lambda bi, h, i: (0, 0))
    return pl.pallas_call(
        functools.partial(_attn_kernel, tq=tq, tk=tk, q_pos0=q_pos0),
        grid=(b, N_HEADS, n_q // tq),
        in_specs=[
            vec(HEAD_DIM), vec(HEAD_DIM), vec(HEAD_DIM), vec(HEAD_DIM), vec(V_DIM),
            pl.BlockSpec((None, tq, V_DIM), q_map),
            pl.BlockSpec((None, seq, V_DIM), lambda bi, h, i: (bi, 0, h_blk + h)),
            pl.BlockSpec((None, seq, V_DIM), lambda bi, h, i: (bi, 0, 2 * h_blk + h)),
            pl.BlockSpec((N_META, V_DIM), lambda bi, h, i: (0, h_blk + h)),
            pl.BlockSpec((N_META, V_DIM), lambda bi, h, i: (0, 2 * h_blk + h)),
        ],
        out_specs=pl.BlockSpec((None, tq, V_DIM), lambda bi, h, i: (bi, i, h)),
        out_shape=jax.ShapeDtypeStruct((b, n_q, N_HEADS * V_DIM), BF16),
        compiler_params=_params("parallel", "parallel", "arbitrary"),
        name="diff_attention",
    )(*[v.reshape(1, HEAD_DIM) for v in lams], subw.reshape(1, V_DIM), yq, y, y, y_meta, y_meta)


def _chan_dft_kernel(f_ref, cs_ref, o_ref):
    t = jnp.dot(f_ref[...], cs_ref[...], preferred_element_type=F32)
    o_ref[0] = t[:, :GROUP_DIM].astype(o_ref.dtype)
    o_ref[1] = t[:, GROUP_DIM:].astype(o_ref.dtype)


def _chan_dft(y, cs, *, tr):
    b, rows, in_width = y.shape
    f_blk0 = in_width // GROUP_DIM - N_GROUPS
    return pl.pallas_call(
        _chan_dft_kernel,
        grid=(b, rows // tr, N_GROUPS),
        in_specs=[
            pl.BlockSpec((None, tr, GROUP_DIM), lambda bi, r, g: (bi, r, f_blk0 + g)),
            pl.BlockSpec((GROUP_DIM, 2 * GROUP_DIM), lambda bi, r, g: (0, 0)),
        ],
        out_specs=pl.BlockSpec((None, 2, tr, GROUP_DIM), lambda bi, r, g: (bi, 0, r, g)),
        out_shape=jax.ShapeDtypeStruct((b, 2, rows, N_GROUPS * GROUP_DIM), BF16),
        compiler_params=_params("parallel", "parallel", "parallel"),
        name="chan_dft",
    )(y, cs)


def _pos_dft_kernel(mr_ref, mm_ref, t_ref, tm_ref, nw_ref, o_ref, *, scale):
    r = jnp.dot(mr_ref[...], t_ref[...], preferred_element_type=F32)
    r = r + jnp.dot(mm_ref[...], tm_ref[...], preferred_element_type=F32)
    r = r * scale
    inv = lax.rsqrt(jnp.mean(r * r, axis=-1, keepdims=True) + NORM_EPS)
    o_ref[...] = (r * inv * nw_ref[...]).astype(o_ref.dtype)


def _pos_dft(mr, mm, t, t_meta, nw, *, tp, scale):
    p_rows = mr.shape[0]
    b, k_real, width = t.shape
    k_meta = t_meta.shape[1]
    return pl.pallas_call(
        functools.partial(_pos_dft_kernel, scale=scale),
        grid=(b, N_GROUPS, p_rows // tp),
        in_specs=[
            pl.BlockSpec((tp, k_real), lambda bi, g, p: (p, 0)),
            pl.BlockSpec((tp, k_meta), lambda bi, g, p: (p, 0)),
            pl.BlockSpec((None, k_real, GROUP_DIM), lambda bi, g, p: (bi, 0, g)),
            pl.BlockSpec((None, k_meta, GROUP_DIM), lambda bi, g, p: (0, 0, g)),
            pl.BlockSpec((1, GROUP_DIM), lambda bi, g, p: (0, g)),
        ],
        out_specs=pl.BlockSpec((None, tp, GROUP_DIM), lambda bi, g, p: (bi, p, g)),
        out_shape=jax.ShapeDtypeStruct((b, p_rows, width), BF16),
        compiler_params=_params("parallel", "parallel", "arbitrary"),
        name="pos_dft",
    )(mr, mm, t, t_meta, nw.reshape(1, width))


def _dft_tables(seq):
    l_total = N_META + seq
    c = jnp.arange(GROUP_DIM, dtype=jnp.int32)
    ang_c = ((c[:, None] * c[None, :]) % GROUP_DIM).astype(F32) * (2.0 * math.pi / GROUP_DIM)
    cs = jnp.concatenate([jnp.cos(ang_c), jnp.sin(ang_c)], axis=1).astype(BF16)

    pos = jnp.arange(l_total, dtype=jnp.int32)
    ang = ((pos[:, None] * pos[None, :]) % l_total).astype(F32) * (2.0 * math.pi / l_total)
    cos_l, sin_l = jnp.cos(ang), jnp.sin(ang)
    real_cols = jnp.concatenate([cos_l[:, N_META:], -sin_l[:, N_META:]], axis=1).astype(BF16)
    meta_cols = jnp.concatenate([cos_l[:, :N_META], -sin_l[:, :N_META]], axis=1).astype(BF16)
    return cs, real_cols, meta_cols


def _matmul_res_kernel(*refs, n_pairs):
    res_ref, o_ref = refs[2 * n_pairs], refs[2 * n_pairs + 1]
    acc = res_ref[...]
    for k in range(n_pairs):
        acc = acc + jnp.dot(refs[2 * k][...], refs[2 * k + 1][...], preferred_element_type=F32)
    o_ref[...] = acc


def _matmul_res(pairs, res, *, tm, tn):
    m, n = res.shape
    in_specs, args = [], []
    for a, w, blk in pairs:
        k = a.shape[1]
        in_specs += [pl.BlockSpec((tm, k), lambda i, j: (i, 0)),
                     pl.BlockSpec((k, tn), lambda i, j, blk=blk: (blk, j))]
        args += [a, w]
    in_specs.append(pl.BlockSpec((tm, tn), lambda i, j: (i, j)))
    return pl.pallas_call(
        functools.partial(_matmul_res_kernel, n_pairs=len(pairs)),
        grid=(m // tm, n // tn),
        in_specs=in_specs,
        out_specs=pl.BlockSpec((tm, tn), lambda i, j: (i, j)),
        out_shape=jax.ShapeDtypeStruct((m, n), F32),
        compiler_params=_params("parallel", "arbitrary"),
        name="matmul_res",
    )(*args, res)


def _glu_kernel(g_ref, v_ref, prev_ref, next_ref, meta_ref, cw_ref, cb_ref, o_ref, *, tr, halo):
    r = pl.program_id(1)
    last = pl.num_programs(1) - 1
    g = g_ref[...].astype(F32)
    prev_row = jnp.where(r == 0, meta_ref[N_META - 1:N_META, :], prev_ref[halo - 1:halo, :]).astype(F32)
    next_row = jnp.where(r == last, 0.0, next_ref[0:1, :].astype(F32))
    row = lax.broadcasted_iota(jnp.int32, g.shape, 0)
    g_prev = jnp.where(row == 0, prev_row, pltpu.roll(g, 1, axis=0))
    g_next = jnp.where(row == tr - 1, next_row, pltpu.roll(g, tr - 1, axis=0))
    conv = g_prev * cw_ref[0:1, :] + g * cw_ref[1:2, :] + g_next * cw_ref[2:3, :] + cb_ref[...]
    gelu = 0.5 * conv * (1.0 + lax.erf(conv * math.sqrt(0.5)))
    o_ref[...] = (gelu * v_ref[...].astype(F32)).astype(o_ref.dtype)


def _glu(u, u_meta_gate, conv_w, conv_b, *, tr, tc):
    b, seq, two_ff = u.shape
    d_ff = two_ff // 2
    n_c = d_ff // tc
    halo = 16
    per = tr // halo
    n_halo = seq // halo
    return pl.pallas_call(
        functools.partial(_glu_kernel, tr=tr, halo=halo),
        grid=(b, seq // tr, n_c),
        in_specs=[
            pl.BlockSpec((None, tr, tc), lambda bi, r, c: (bi, r, c)),
            pl.BlockSpec((None, tr, tc), lambda bi, r, c: (bi, r, n_c + c)),
            pl.BlockSpec((None, halo, tc), lambda bi, r, c: (bi, jnp.maximum(r * per - 1, 0), c)),
            pl.BlockSpec((None, halo, tc),
                         lambda bi, r, c: (bi, jnp.minimum((r + 1) * per, n_halo - 1), c)),
            pl.BlockSpec((None, N_META, tc), lambda bi, r, c: (bi, 0, c)),
            pl.BlockSpec((CONV_WIDTH, tc), lambda bi, r, c: (0, c)),
            pl.BlockSpec((1, tc), lambda bi, r, c: (0, c)),
        ],
        out_specs=pl.BlockSpec((None, tr, tc), lambda bi, r, c: (bi, r, c)),
        out_shape=jax.ShapeDtypeStruct((b, seq, d_ff), BF16),
        compiler_params=_params("parallel", "parallel", "parallel"),
        name="conv_glu",
    )(u, u, u, u, u_meta_gate, conv_w, conv_b.reshape(1, d_ff))


def _rmsnorm_kernel(x_ref, w_ref, o_ref):
    x = x_ref[...]
    inv = lax.rsqrt(jnp.mean(x * x, axis=-1, keepdims=True) + NORM_EPS)
    o_ref[...] = x * inv * w_ref[...]


def _rmsnorm(x, w, *, tm):
    m, d = x.shape
    return pl.pallas_call(
        _rmsnorm_kernel,
        grid=(m // tm,),
        in_specs=[pl.BlockSpec((tm, d), lambda i: (i, 0)), pl.BlockSpec((1, d), lambda i: (0, 0))],
        out_specs=pl.BlockSpec((tm, d), lambda i: (i, 0)),
        out_shape=jax.ShapeDtypeStruct((m, d), F32),
        compiler_params=_params("parallel"),
        name="final_rmsnorm",
    )(x, w.reshape(1, d))


def kernel(x, meta_tokens, norm1_w, w_in, lambda_q1, lambda_k1, lambda_q2, lambda_k2,
           diff_subln_w, fourier_norm_w, w_out, norm2_w, w_up, conv_w, conv_b, w_down,
           final_norm_w):
    b, seq, d = x.shape
    m = b * seq
    in_width = w_in.shape[-1]
    d_ff = w_down.shape[1]
    assert w_in.shape[0] == 1 and meta_tokens.shape[0] == N_META

    w_in_b = w_in[0].astype(BF16)
    w_out_b = w_out[0].astype(BF16)
    w_up_b = w_up[0].astype(BF16)
    w_down_b = w_down[0].astype(BF16)
    lams = (lambda_q1[0], lambda_k1[0], lambda_q2[0], lambda_k2[0])

    y = _norm_matmul(x.reshape(m, d), norm1_w[0], w_in_b, tm=512, tn=1024).reshape(b, seq, in_width)
    y_meta = _norm_matmul(meta_tokens, norm1_w[0], w_in_b, tm=N_META, tn=1024)

    attn = _attention(lams, diff_subln_w[0], y, y, y_meta, tq=256, tk=512,
                      q_pos0=N_META, q_batched=True)
    attn_meta = _attention(lams, diff_subln_w[0], y_meta[None], y, y_meta, tq=N_META, tk=512,
                           q_pos0=0, q_batched=False)

    cs, real_cols, meta_cols = _dft_tables(seq)
    four_w = N_GROUPS * GROUP_DIM
    t = _chan_dft(y, cs, tr=min(1024, seq)).reshape(b, 2 * seq, four_w)
    t_meta = _chan_dft(y_meta[None], cs, tr=N_META).reshape(1, 2 * N_META, four_w)
    scale = 1.0 / math.sqrt((N_META + seq) * GROUP_DIM)
    four = _pos_dft(real_cols[N_META:], meta_cols[N_META:], t, t_meta, fourier_norm_w[0],
                    tp=512, scale=scale)
    four_meta = _pos_dft(real_cols[:N_META], meta_cols[:N_META], t, t_meta, fourier_norm_w[0],
                         tp=N_META, scale=scale)

    attn_w = attn.shape[-1]
    h_mid = _matmul_res([(attn.reshape(m, attn_w), w_out_b, 0), (four.reshape(m, four_w), w_out_b, 1)],
                        x.reshape(m, d), tm=512, tn=min(1024, d))
    meta_res = jnp.broadcast_to(meta_tokens[None], (b, N_META, d)).reshape(b * N_META, d)
    h_mid_meta = _matmul_res([(attn_meta.reshape(b * N_META, attn_w), w_out_b, 0),
                              (four_meta.reshape(b * N_META, four_w), w_out_b, 1)],
                             meta_res, tm=b * N_META, tn=min(1024, d))

    u = _norm_matmul(h_mid, norm2_w[0], w_up_b, tm=512, tn=512).reshape(b, seq, 2 * d_ff)
    u_meta_gate = _norm_matmul(h_mid_meta, norm2_w[0], w_up_b, tm=b * N_META, tn=256,
                               n_cols=d_ff).reshape(b, N_META, d_ff)
    z = _glu(u, u_meta_gate, conv_w[0], conv_b[0], tr=min(2048, seq), tc=256)
    h_out = _matmul_res([(z.reshape(m, d_ff), w_down_b, 0)], h_mid, tm=512, tn=min(512, d))

    return _rmsnorm(h_out, final_norm_w, tm=512).reshape(b, seq, d)
```

```python
import functools
import math

import jax
import jax.numpy as jnp
from jax import lax
from jax.experimental import pallas as pl
from jax.experimental.pallas import tpu as pltpu

N_META = 16
N_HEADS = 16
HEAD_DIM = 64
V_DIM = 128
N_GROUPS = 4
GROUP_DIM = 512
CONV_WIDTH = 3
NORM_EPS = 1e-6
SUBLN_EPS = 1e-5
LAMBDA_INIT = 0.8 - 0.6 * math.exp(-0.3 * 0)

VMEM_LIMIT_BYTES = 56 * 1024 * 1024
BF16 = jnp.bfloat16
F32 = jnp.float32


def _params(*sem):
    return pltpu.CompilerParams(dimension_semantics=sem, vmem_limit_bytes=VMEM_LIMIT_BYTES)


def _norm_matmul_kernel(x_ref, nw_ref, w_ref, o_ref, hn_ref):
    @pl.when(pl.program_id(1) == 0)
    def _():
        x = x_ref[...]
        inv = lax.rsqrt(jnp.mean(x * x, axis=-1, keepdims=True) + NORM_EPS)
        hn_ref[...] = (x * inv * nw_ref[...]).astype(BF16)

    o_ref[...] = jnp.dot(hn_ref[...], w_ref[...], preferred_element_type=F32).astype(o_ref.dtype)


def _norm_matmul(x, nw, w, *, tm, tn, n_cols=None):
    m, d = x.shape
    n = w.shape[1] if n_cols is None else n_cols
    return pl.pallas_call(
        _norm_matmul_kernel,
        grid=(m // tm, n // tn),
        in_specs=[
            pl.BlockSpec((tm, d), lambda i, j: (i, 0)),
            pl.BlockSpec((1, d), lambda i, j: (0, 0)),
            pl.BlockSpec((d, tn), lambda i, j: (0, j)),
        ],
        out_specs=pl.BlockSpec((tm, tn), lambda i, j: (i, j)),
        out_shape=jax.ShapeDtypeStruct((m, n), BF16),
        scratch_shapes=[pltpu.VMEM((tm, d), BF16)],
        compiler_params=_params("parallel", "arbitrary"),
        name="norm_matmul",
    )(x, nw.reshape(1, d), w)


META_PAD = 128
SUM_ROWS = 16
MASKED = -1e30
LOG2E = 1.0 / math.log(2.0)
BEFORE, DIAG, AFTER = 0, 1, 2


def _attn_kernel(lq1_ref, lk1_ref, lq2_ref, lk2_ref, subw_ref, q_ref, qm_ref, k_ref, v_ref,
                 km_ref, vm_ref, o_ref, om_ref, vt_ref, vtm_ref, bias_ref, ta_ref, tb_ref, *, t):
    h = pl.program_id(1)
    qi = pl.program_id(2)
    seq = k_ref.shape[0]
    n_chunks = seq // t
    acc_rows = V_DIM + SUM_ROWS

    slope = LOG2E * jnp.exp2(jnp.full((1, 1), -8.0 / N_HEADS, F32) * (h + 1).astype(F32))

    @pl.when(qi == 0)
    def _():
        for c in range(n_chunks):
            vt_ref[0:V_DIM, c * t:(c + 1) * t] = v_ref[c * t:(c + 1) * t, :].astype(F32).T.astype(BF16)
        vt_ref[V_DIM:acc_rows, :] = jnp.ones((SUM_ROWS, seq), BF16)
        vtm_ref[0:V_DIM, :] = vm_ref[...].astype(F32).T.astype(BF16)
        vtm_ref[V_DIM:acc_rows, :] = jnp.ones((SUM_ROWS, META_PAD), BF16)
        row = lax.broadcasted_iota(jnp.int32, (t, t), 0)
        col = lax.broadcasted_iota(jnp.int32, (t, t), 1)
        bias_ref[BEFORE] = slope * row.astype(F32)
        bias_ref[DIAG] = -slope * jnp.abs(col - row).astype(F32)
        bias_ref[AFTER] = -slope * row.astype(F32)

    lam = (jnp.exp(jnp.sum(lq1_ref[...] * lk1_ref[...], axis=-1, keepdims=True))
           - jnp.exp(jnp.sum(lq2_ref[...] * lk2_ref[...], axis=-1, keepdims=True))
           + LAMBDA_INIT)
    lane = lax.broadcasted_iota(jnp.int32, (1, V_DIM), 1)
    meta_row = lax.broadcasted_iota(jnp.int32, (META_PAD, 1), 0)
    meta_valid = meta_row < N_META

    def attend(q, q_pos, tile_idx, out_ref):
        nq = q.shape[0]
        q32 = q.astype(F32) * (LOG2E / math.sqrt(HEAD_DIM))
        qs = (jnp.where(lane < HEAD_DIM, q32, 0.0).astype(BF16),
              jnp.where(lane >= HEAD_DIM, q32, 0.0).astype(BF16))

        def scores(k_c):
            return tuple(lax.dot_general(k_c, qc, (((1,), (1,)), ((), ())), preferred_element_type=F32)
                         for qc in qs)

        def update(m, acc, tt, tt_max, row_bias, vt_c):
            m_new = jnp.maximum(m, tt_max + row_bias)
            p = jnp.exp2(tt - (m_new - row_bias))
            acc = jnp.exp2(m - m_new) * acc + jnp.dot(vt_c, p.astype(BF16), preferred_element_type=F32)
            return m_new, acc

        if tile_idx is None:
            key_bias, row_bias = -slope * jnp.abs(q_pos - meta_row.astype(F32)), jnp.zeros((1, nq), F32)
        else:
            key_bias, row_bias = bias_ref[BEFORE, 0:META_PAD, 0:nq], -slope * q_pos
        init_m, init_acc = jnp.full((1, nq), -jnp.inf, F32), jnp.zeros((acc_rows, nq), F32)
        stats = []
        for s in scores(km_ref[...]):
            tt = jnp.where(meta_valid, s + key_bias, MASKED)
            stats.append(update(init_m, init_acc, tt, jnp.max(tt, axis=0, keepdims=True), row_bias,
                                vtm_ref[...]))
        stats = tuple(stats)

        def chunk_kind(c_idx):
            if tile_idx is None:
                return AFTER
            return (c_idx >= tile_idx).astype(jnp.int32) + (c_idx > tile_idx).astype(jnp.int32)

        def produce(c_idx, buf_ref):
            k_c = k_ref[pl.ds(pl.multiple_of(c_idx * t, t), t), :]
            key_bias = bias_ref[chunk_kind(c_idx), :, 0:nq]
            maxes = []
            for c, s in enumerate(scores(k_c)):
                tt = s + key_bias
                buf_ref[c, :, 0:nq] = tt
                maxes.append(jnp.max(tt, axis=0, keepdims=True))
            return tuple(maxes)

        def consume(c_idx, buf_ref, stats, maxes):
            sign = 1.0 - chunk_kind(c_idx)
            k_pos0 = (N_META + c_idx * t).astype(F32)
            row_bias = (-sign) * slope * (q_pos - k_pos0)
            vt_c = vt_ref[:, pl.ds(pl.multiple_of(c_idx * t, t), t)]
            return tuple(update(m, acc, buf_ref[c, :, 0:nq], tt_max, row_bias, vt_c)
                         for c, ((m, acc), tt_max) in enumerate(zip(stats, maxes)))

        def body(i, carry):
            stats, max_a = carry
            max_b = produce(2 * i + 1, tb_ref)
            stats = consume(2 * i, ta_ref, stats, max_a)
            max_a = produce(2 * i + 2, ta_ref)
            stats = consume(2 * i + 1, tb_ref, stats, max_b)
            return stats, max_a

        stats, max_a = lax.fori_loop(0, n_chunks // 2 - 1, body, (stats, produce(jnp.int32(0), ta_ref)))
        max_b = produce(jnp.int32(n_chunks - 1), tb_ref)
        stats = consume(jnp.int32(n_chunks - 2), ta_ref, stats, max_a)
        stats = consume(jnp.int32(n_chunks - 1), tb_ref, stats, max_b)

        (_, acc1), (_, acc2) = stats
        a = acc1[0:V_DIM] / acc1[V_DIM:V_DIM + 1] - lam * (acc2[0:V_DIM] / acc2[V_DIM:V_DIM + 1])
        inv = lax.rsqrt(jnp.mean(a * a, axis=0, keepdims=True) + SUBLN_EPS)
        out_ref[...] = ((a * inv).T * subw_ref[...] * (1.0 - LAMBDA_INIT)).astype(out_ref.dtype)

    q_pos = (N_META + qi * t + lax.broadcasted_iota(jnp.int32, (1, t), 1)).astype(F32)
    attend(q_ref[...], q_pos, qi, o_ref)

    @pl.when(qi == 0)
    def _():
        pos = jnp.minimum(lax.broadcasted_iota(jnp.int32, (1, META_PAD), 1), N_META - 1)
        attend(qm_ref[...], pos.astype(F32), None, om_ref)


def _attention(lams, subw, y, y_meta_pad, *, t):
    b, seq, _ = y.shape
    h_blk = N_HEADS
    vec = lambda n: pl.BlockSpec((1, n), lambda bi, h, i: (0, 0))
    acc_rows = V_DIM + SUM_ROWS
    return pl.pallas_call(
        functools.partial(_attn_kernel, t=t),
        grid=(b, N_HEADS, seq // t),
        in_specs=[
            vec(HEAD_DIM), vec(HEAD_DIM), vec(HEAD_DIM), vec(HEAD_DIM), vec(V_DIM),
            pl.BlockSpec((None, t, V_DIM), lambda bi, h, i: (bi, i, h)),
            pl.BlockSpec((META_PAD, V_DIM), lambda bi, h, i: (0, h)),
            pl.BlockSpec((None, seq, V_DIM), lambda bi, h, i: (bi, 0, h_blk + h)),
            pl.BlockSpec((None, seq, V_DIM), lambda bi, h, i: (bi, 0, 2 * h_blk + h)),
            pl.BlockSpec((META_PAD, V_DIM), lambda bi, h, i: (0, h_blk + h)),
            pl.BlockSpec((META_PAD, V_DIM), lambda bi, h, i: (0, 2 * h_blk + h)),
        ],
        out_specs=[
            pl.BlockSpec((None, t, V_DIM), lambda bi, h, i: (bi, i, h)),
            pl.BlockSpec((None, META_PAD, V_DIM), lambda bi, h, i: (bi, 0, h)),
        ],
        out_shape=[
            jax.ShapeDtypeStruct((b, seq, N_HEADS * V_DIM), BF16),
            jax.ShapeDtypeStruct((b, META_PAD, N_HEADS * V_DIM), BF16),
        ],
        scratch_shapes=[
            pltpu.VMEM((acc_rows, seq), BF16),
            pltpu.VMEM((acc_rows, META_PAD), BF16),
            pltpu.VMEM((3, t, t), F32),
            pltpu.VMEM((2, t, t), F32),
            pltpu.VMEM((2, t, t), F32),
        ],
        compiler_params=_params("parallel", "parallel", "arbitrary"),
        name="diff_attention",
    )(*[v.reshape(1, HEAD_DIM) for v in lams], subw.reshape(1, V_DIM),
      y, y_meta_pad, y, y, y_meta_pad, y_meta_pad)


def _chan_dft_kernel(f_ref, cs_ref, o_ref):
    t = jnp.dot(f_ref[...], cs_ref[...], preferred_element_type=F32)
    o_ref[0] = t[:, :GROUP_DIM].astype(o_ref.dtype)
    o_ref[1] = t[:, GROUP_DIM:].astype(o_ref.dtype)


def _chan_dft(y, cs, *, tr):
    b, rows, in_width = y.shape
    f_blk0 = in_width // GROUP_DIM - N_GROUPS
    return pl.pallas_call(
        _chan_dft_kernel,
        grid=(b, rows // tr, N_GROUPS),
        in_specs=[
            pl.BlockSpec((None, tr, GROUP_DIM), lambda bi, r, g: (bi, r, f_blk0 + g)),
            pl.BlockSpec((GROUP_DIM, 2 * GROUP_DIM), lambda bi, r, g: (0, 0)),
        ],
        out_specs=pl.BlockSpec((None, 2, tr, GROUP_DIM), lambda bi, r, g: (bi, 0, r, g)),
        out_shape=jax.ShapeDtypeStruct((b, 2, rows, N_GROUPS * GROUP_DIM), BF16),
        compiler_params=_params("parallel", "parallel", "parallel"),
        name="chan_dft",
    )(y, cs)


def _pos_dft_kernel(mr_ref, mm_ref, t_ref, tm_ref, nw_ref, o_ref, *, scale):
    r = jnp.dot(mr_ref[...], t_ref[...], preferred_element_type=F32)
    r = r + jnp.dot(mm_ref[...], tm_ref[...], preferred_element_type=F32)
    r = r * scale
    inv = lax.rsqrt(jnp.mean(r * r, axis=-1, keepdims=True) + NORM_EPS)
    o_ref[...] = (r * inv * nw_ref[...]).astype(o_ref.dtype)


def _pos_dft(mr, mm, t, t_meta, nw, *, tp, scale):
    p_rows = mr.shape[0]
    b, k_real, width = t.shape
    k_meta = t_meta.shape[1]
    return pl.pallas_call(
        functools.partial(_pos_dft_kernel, scale=scale),
        grid=(b, N_GROUPS, p_rows // tp),
        in_specs=[
            pl.BlockSpec((tp, k_real), lambda bi, g, p: (p, 0)),
            pl.BlockSpec((tp, k_meta), lambda bi, g, p: (p, 0)),
            pl.BlockSpec((None, k_real, GROUP_DIM), lambda bi, g, p: (bi, 0, g)),
            pl.BlockSpec((None, k_meta, GROUP_DIM), lambda bi, g, p: (0, 0, g)),
            pl.BlockSpec((1, GROUP_DIM), lambda bi, g, p: (0, g)),
        ],
        out_specs=pl.BlockSpec((None, tp, GROUP_DIM), lambda bi, g, p: (bi, p, g)),
        out_shape=jax.ShapeDtypeStruct((b, p_rows, width), BF16),
        compiler_params=_params("parallel", "parallel", "arbitrary"),
        name="pos_dft",
    )(mr, mm, t, t_meta, nw.reshape(1, width))


def _dft_tables(seq):
    l_total = N_META + seq
    c = jnp.arange(GROUP_DIM, dtype=jnp.int32)
    ang_c = ((c[:, None] * c[None, :]) % GROUP_DIM).astype(F32) * (2.0 * math.pi / GROUP_DIM)
    cs = jnp.concatenate([jnp.cos(ang_c), jnp.sin(ang_c)], axis=1).astype(BF16)

    pos = jnp.arange(l_total, dtype=jnp.int32)
    ang = ((pos[:, None] * pos[None, :]) % l_total).astype(F32) * (2.0 * math.pi / l_total)
    cos_l, sin_l = jnp.cos(ang), jnp.sin(ang)
    real_cols = jnp.concatenate([cos_l[:, N_META:], -sin_l[:, N_META:]], axis=1).astype(BF16)
    meta_cols = jnp.concatenate([cos_l[:, :N_META], -sin_l[:, :N_META]], axis=1).astype(BF16)
    return cs, real_cols, meta_cols


def _matmul_res_kernel(*refs, n_pairs):
    res_ref, o_ref = refs[2 * n_pairs], refs[2 * n_pairs + 1]
    acc = res_ref[...]
    for k in range(n_pairs):
        acc = acc + jnp.dot(refs[2 * k][...], refs[2 * k + 1][...], preferred_element_type=F32)
    o_ref[...] = acc


def _matmul_res(pairs, res, *, tm, tn):
    m, n = res.shape
    in_specs, args = [], []
    for a, w, blk in pairs:
        k = a.shape[1]
        in_specs += [pl.BlockSpec((tm, k), lambda i, j: (i, 0)),
                     pl.BlockSpec((k, tn), lambda i, j, blk=blk: (blk, j))]
        args += [a, w]
    in_specs.append(pl.BlockSpec((tm, tn), lambda i, j: (i, j)))
    return pl.pallas_call(
        functools.partial(_matmul_res_kernel, n_pairs=len(pairs)),
        grid=(m // tm, n // tn),
        in_specs=in_specs,
        out_specs=pl.BlockSpec((tm, tn), lambda i, j: (i, j)),
        out_shape=jax.ShapeDtypeStruct((m, n), F32),
        compiler_params=_params("parallel", "arbitrary"),
        name="matmul_res",
    )(*args, res)


def _glu_kernel(g_ref, v_ref, prev_ref, next_ref, meta_ref, cw_ref, cb_ref, o_ref, *, tr, halo):
    r = pl.program_id(1)
    last = pl.num_programs(1) - 1
    g = g_ref[...].astype(F32)
    prev_row = jnp.where(r == 0, meta_ref[N_META - 1:N_META, :], prev_ref[halo - 1:halo, :]).astype(F32)
    next_row = jnp.where(r == last, 0.0, next_ref[0:1, :].astype(F32))
    row = lax.broadcasted_iota(jnp.int32, g.shape, 0)
    g_prev = jnp.where(row == 0, prev_row, pltpu.roll(g, 1, axis=0))
    g_next = jnp.where(row == tr - 1, next_row, pltpu.roll(g, tr - 1, axis=0))
    conv = g_prev * cw_ref[0:1, :] + g * cw_ref[1:2, :] + g_next * cw_ref[2:3, :] + cb_ref[...]
    gelu = 0.5 * conv * (1.0 + lax.erf(conv * math.sqrt(0.5)))
    o_ref[...] = (gelu * v_ref[...].astype(F32)).astype(o_ref.dtype)


def _glu(u, u_meta_gate, conv_w, conv_b, *, tr, tc):
    b, seq, two_ff = u.shape
    d_ff = two_ff // 2
    n_c = d_ff // tc
    halo = 16
    per = tr // halo
    n_halo = seq // halo
    return pl.pallas_call(
        functools.partial(_glu_kernel, tr=tr, halo=halo),
        grid=(b, seq // tr, n_c),
        in_specs=[
            pl.BlockSpec((None, tr, tc), lambda bi, r, c: (bi, r, c)),
            pl.BlockSpec((None, tr, tc), lambda bi, r, c: (bi, r, n_c + c)),
            pl.BlockSpec((None, halo, tc), lambda bi, r, c: (bi, jnp.maximum(r * per - 1, 0), c)),
            pl.BlockSpec((None, halo, tc),
                         lambda bi, r, c: (bi, jnp.minimum((r + 1) * per, n_halo - 1), c)),
            pl.BlockSpec((None, N_META, tc), lambda bi, r, c: (bi, 0, c)),
            pl.BlockSpec((CONV_WIDTH, tc), lambda bi, r, c: (0, c)),
            pl.BlockSpec((1, tc), lambda bi, r, c: (0, c)),
        ],
        out_specs=pl.BlockSpec((None, tr, tc), lambda bi, r, c: (bi, r, c)),
        out_shape=jax.ShapeDtypeStruct((b, seq, d_ff), BF16),
        compiler_params=_params("parallel", "parallel", "parallel"),
        name="conv_glu",
    )(u, u, u, u, u_meta_gate, conv_w, conv_b.reshape(1, d_ff))


def _rmsnorm_kernel(x_ref, w_ref, o_ref):
    x = x_ref[...]
    inv = lax.rsqrt(jnp.mean(x * x, axis=-1, keepdims=True) + NORM_EPS)
    o_ref[...] = x * inv * w_ref[...]


def _rmsnorm(x, w, *, tm):
    m, d = x.shape
    return pl.pallas_call(
        _rmsnorm_kernel,
        grid=(m // tm,),
        in_specs=[pl.BlockSpec((tm, d), lambda i: (i, 0)), pl.BlockSpec((1, d), lambda i: (0, 0))],
        out_specs=pl.BlockSpec((tm, d), lambda i: (i, 0)),
        out_shape=jax.ShapeDtypeStruct((m, d), F32),
        compiler_params=_params("parallel"),
        name="final_rmsnorm",
    )(x, w.reshape(1, d))


def kernel(x, meta_tokens, norm1_w, w_in, lambda_q1, lambda_k1, lambda_q2, lambda_k2,
           diff_subln_w, fourier_norm_w, w_out, norm2_w, w_up, conv_w, conv_b, w_down,
           final_norm_w):
    b, seq, d = x.shape
    m = b * seq
    in_width = w_in.shape[-1]
    d_ff = w_down.shape[1]
    assert w_in.shape[0] == 1 and meta_tokens.shape[0] == N_META

    w_in_b = w_in[0].astype(BF16)
    w_out_b = w_out[0].astype(BF16)
    w_up_b = w_up[0].astype(BF16)
    w_down_b = w_down[0].astype(BF16)
    lams = (lambda_q1[0], lambda_k1[0], lambda_q2[0], lambda_k2[0])

    y = _norm_matmul(x.reshape(m, d), norm1_w[0], w_in_b, tm=512, tn=1024).reshape(b, seq, in_width)
    y_meta = _norm_matmul(meta_tokens, norm1_w[0], w_in_b, tm=N_META, tn=1024)

    y_meta_pad = jnp.pad(y_meta, ((0, META_PAD - N_META), (0, 0)))
    attn, attn_meta = _attention(lams, diff_subln_w[0], y, y_meta_pad, t=min(512, seq))
    attn_meta = attn_meta[:, :N_META]

    cs, real_cols, meta_cols = _dft_tables(seq)
    four_w = N_GROUPS * GROUP_DIM
    t = _chan_dft(y, cs, tr=min(1024, seq)).reshape(b, 2 * seq, four_w)
    t_meta = _chan_dft(y_meta[None], cs, tr=N_META).reshape(1, 2 * N_META, four_w)
    scale = 1.0 / math.sqrt((N_META + seq) * GROUP_DIM)
    four = _pos_dft(real_cols[N_META:], meta_cols[N_META:], t, t_meta, fourier_norm_w[0],
                    tp=512, scale=scale)
    four_meta = _pos_dft(real_cols[:N_META], meta_cols[:N_META], t, t_meta, fourier_norm_w[0],
                         tp=N_META, scale=scale)

    attn_w = attn.shape[-1]
    h_mid = _matmul_res([(attn.reshape(m, attn_w), w_out_b, 0), (four.reshape(m, four_w), w_out_b, 1)],
                        x.reshape(m, d), tm=512, tn=min(1024, d))
    meta_res = jnp.broadcast_to(meta_tokens[None], (b, N_META, d)).reshape(b * N_META, d)
    h_mid_meta = _matmul_res([(attn_meta.reshape(b * N_META, attn_w), w_out_b, 0),
                              (four_meta.reshape(b * N_META, four_w), w_out_b, 1)],
                             meta_res, tm=b * N_META, tn=min(1024, d))

    u = _norm_matmul(h_mid, norm2_w[0], w_up_b, tm=512, tn=512).reshape(b, seq, 2 * d_ff)
    u_meta_gate = _norm_matmul(h_mid_meta, norm2_w[0], w_up_b, tm=b * N_META, tn=256,
                               n_cols=d_ff).reshape(b, N_META, d_ff)
    z = _glu(u, u_meta_gate, conv_w[0], conv_b[0], tr=min(2048, seq), tc=256)
    h_out = _matmul_res([(z.reshape(m, d_ff), w_down_b, 0)], h_mid, tm=512, tn=min(512, d))

    return _rmsnorm(h_out, final_norm_w, tm=512).reshape(b, seq, d)
```

```python
import functools
import math

import jax
import jax.numpy as jnp
from jax import lax
from jax.experimental import pallas as pl
from jax.experimental.pallas import tpu as pltpu

N_META = 16
N_HEADS = 16
HEAD_DIM = 64
V_DIM = 128
N_GROUPS = 4
GROUP_DIM = 512
CONV_WIDTH = 3
NORM_EPS = 1e-6
SUBLN_EPS = 1e-5
LAMBDA_INIT = 0.8 - 0.6 * math.exp(-0.3 * 0)

VMEM_LIMIT_BYTES = 56 * 1024 * 1024
BF16 = jnp.bfloat16
F32 = jnp.float32


def _params(*sem):
    return pltpu.CompilerParams(dimension_semantics=sem, vmem_limit_bytes=VMEM_LIMIT_BYTES)


def _norm_matmul_kernel(x_ref, nw_ref, w_ref, o_ref, hn_ref):
    @pl.when(pl.program_id(1) == 0)
    def _():
        x = x_ref[...]
        inv = lax.rsqrt(jnp.mean(x * x, axis=-1, keepdims=True) + NORM_EPS)
        hn_ref[...] = (x * inv * nw_ref[...]).astype(BF16)

    o_ref[...] = jnp.dot(hn_ref[...], w_ref[...], preferred_element_type=F32).astype(o_ref.dtype)


def _norm_matmul(x, nw, w, *, tm, tn, n_cols=None):
    m, d = x.shape
    n = w.shape[1] if n_cols is None else n_cols
    return pl.pallas_call(
        _norm_matmul_kernel,
        grid=(m // tm, n // tn),
        in_specs=[
            pl.BlockSpec((tm, d), lambda i, j: (i, 0)),
            pl.BlockSpec((1, d), lambda i, j: (0, 0)),
            pl.BlockSpec((d, tn), lambda i, j: (0, j)),
        ],
        out_specs=pl.BlockSpec((tm, tn), lambda i, j: (i, j)),
        out_shape=jax.ShapeDtypeStruct((m, n), BF16),
        scratch_shapes=[pltpu.VMEM((tm, d), BF16)],
        compiler_params=_params("parallel", "arbitrary"),
        name="norm_matmul",
    )(x, nw.reshape(1, d), w)


META_PAD = 128
SUM_ROWS = 16
MASKED = -1e30
LOG2E = 1.0 / math.log(2.0)
BEFORE, DIAG, AFTER = 0, 1, 2


def _attn_kernel(lq1_ref, lk1_ref, lq2_ref, lk2_ref, subw_ref, q_ref, qm_ref, k_ref, v_ref,
                 km_ref, vm_ref, o_ref, om_ref, vt_ref, vtm_ref, bias_ref, ta_ref, tb_ref, *, t):
    h = pl.program_id(1)
    qi = pl.program_id(2)
    seq = k_ref.shape[0]
    n_chunks = seq // t
    acc_rows = V_DIM + SUM_ROWS

    slope = LOG2E * jnp.exp2(jnp.full((1, 1), -8.0 / N_HEADS, F32) * jnp.asarray(h + 1, F32))

    @pl.when(qi == 0)
    def _():
        for c in range(n_chunks):
            vt_ref[0:V_DIM, c * t:(c + 1) * t] = v_ref[c * t:(c + 1) * t, :].astype(F32).T.astype(BF16)
        vt_ref[V_DIM:acc_rows, :] = jnp.ones((SUM_ROWS, seq), BF16)
        vtm_ref[0:V_DIM, :] = vm_ref[...].astype(F32).T.astype(BF16)
        vtm_ref[V_DIM:acc_rows, :] = jnp.ones((SUM_ROWS, META_PAD), BF16)
        row = lax.broadcasted_iota(jnp.int32, (t, t), 0)
        col = lax.broadcasted_iota(jnp.int32, (t, t), 1)
        bias_ref[BEFORE] = slope * row.astype(F32)
        bias_ref[DIAG] = -slope * jnp.abs(col - row).astype(F32)
        bias_ref[AFTER] = -slope * row.astype(F32)

    lam = (jnp.exp(jnp.sum(lq1_ref[...] * lk1_ref[...], axis=-1, keepdims=True))
           - jnp.exp(jnp.sum(lq2_ref[...] * lk2_ref[...], axis=-1, keepdims=True))
           + LAMBDA_INIT)
    lane = lax.broadcasted_iota(jnp.int32, (1, V_DIM), 1)
    meta_row = lax.broadcasted_iota(jnp.int32, (META_PAD, 1), 0)
    meta_valid = meta_row < N_META

    def attend(q, q_pos, tile_idx, out_ref):
        nq = q.shape[0]
        q32 = q.astype(F32) * (LOG2E / math.sqrt(HEAD_DIM))
        qs = (jnp.where(lane < HEAD_DIM, q32, 0.0).astype(BF16),
              jnp.where(lane >= HEAD_DIM, q32, 0.0).astype(BF16))

        def scores(k_c):
            return tuple(lax.dot_general(k_c, qc, (((1,), (1,)), ((), ())), preferred_element_type=F32)
                         for qc in qs)

        def update(m, acc, tt, tt_max, row_bias, vt_c):
            m_new = jnp.maximum(m, tt_max + row_bias)
            p = jnp.exp2(tt - (m_new - row_bias))
            acc = jnp.exp2(m - m_new) * acc + jnp.dot(vt_c, p.astype(BF16), preferred_element_type=F32)
            return m_new, acc

        if tile_idx is None:
            key_bias, row_bias = -slope * jnp.abs(q_pos - meta_row.astype(F32)), jnp.zeros((1, nq), F32)
        else:
            key_bias, row_bias = bias_ref[BEFORE, 0:META_PAD, 0:nq], -slope * q_pos
        init_m, init_acc = jnp.full((1, nq), -jnp.inf, F32), jnp.zeros((acc_rows, nq), F32)
        stats = []
        for s in scores(km_ref[...]):
            tt = jnp.where(meta_valid, s + key_bias, MASKED)
            stats.append(update(init_m, init_acc, tt, jnp.max(tt, axis=0, keepdims=True), row_bias,
                                vtm_ref[...]))
        stats = tuple(stats)

        def chunk_kind(c_idx):
            if tile_idx is None:
                return AFTER
            return jnp.asarray(c_idx >= tile_idx, jnp.int32) + jnp.asarray(c_idx > tile_idx, jnp.int32)

        def produce(c_idx, buf_ref):
            k_c = k_ref[pl.ds(pl.multiple_of(c_idx * t, t), t), :]
            key_bias = bias_ref[chunk_kind(c_idx), :, 0:nq]
            maxes = []
            for c, s in enumerate(scores(k_c)):
                tt = s + key_bias
                buf_ref[c, :, 0:nq] = tt
                maxes.append(jnp.max(tt, axis=0, keepdims=True))
            return tuple(maxes)

        def consume(c_idx, buf_ref, stats, maxes):
            sign = 1.0 - chunk_kind(c_idx)
            k_pos0 = jnp.asarray(N_META + c_idx * t, F32)
            row_bias = (-sign) * slope * (q_pos - k_pos0)
            vt_c = vt_ref[:, pl.ds(pl.multiple_of(c_idx * t, t), t)]
            return tuple(update(m, acc, buf_ref[c, :, 0:nq], tt_max, row_bias, vt_c)
                         for c, ((m, acc), tt_max) in enumerate(zip(stats, maxes)))

        def body(i, carry):
            stats, max_a = carry
            max_b = produce(2 * i + 1, tb_ref)
            stats = consume(2 * i, ta_ref, stats, max_a)
            max_a = produce(2 * i + 2, ta_ref)
            stats = consume(2 * i + 1, tb_ref, stats, max_b)
            return stats, max_a

        stats, max_a = lax.fori_loop(0, n_chunks // 2 - 1, body, (stats, produce(jnp.int32(0), ta_ref)))
        max_b = produce(jnp.int32(n_chunks - 1), tb_ref)
        stats = consume(jnp.int32(n_chunks - 2), ta_ref, stats, max_a)
        stats = consume(jnp.int32(n_chunks - 1), tb_ref, stats, max_b)

        (_, acc1), (_, acc2) = stats
        a = acc1[0:V_DIM] / acc1[V_DIM:V_DIM + 1] - lam * (acc2[0:V_DIM] / acc2[V_DIM:V_DIM + 1])
        inv = lax.rsqrt(jnp.mean(a * a, axis=0, keepdims=True) + SUBLN_EPS)
        out_ref[...] = ((a * inv).T * subw_ref[...] * (1.0 - LAMBDA_INIT)).astype(out_ref.dtype)

    q_pos = (N_META + qi * t + lax.broadcasted_iota(jnp.int32, (1, t), 1)).astype(F32)
    attend(q_ref[...], q_pos, qi, o_ref)

    @pl.when(qi == 0)
    def _():
        pos = jnp.minimum(lax.broadcasted_iota(jnp.int32, (1, META_PAD), 1), N_META - 1)
        attend(qm_ref[...], pos.astype(F32), None, om_ref)


def _attention(lams, subw, y, y_meta_pad, *, t):
    b, seq, _ = y.shape
    h_blk = N_HEADS
    vec = lambda n: pl.BlockSpec((1, n), lambda bi, h, i: (0, 0))
    acc_rows = V_DIM + SUM_ROWS
    return pl.pallas_call(
        functools.partial(_attn_kernel, t=t),
        grid=(b, N_HEADS, seq // t),
        in_specs=[
            vec(HEAD_DIM), vec(HEAD_DIM), vec(HEAD_DIM), vec(HEAD_DIM), vec(V_DIM),
            pl.BlockSpec((None, t, V_DIM), lambda bi, h, i: (bi, i, h)),
            pl.BlockSpec((META_PAD, V_DIM), lambda bi, h, i: (0, h)),
            pl.BlockSpec((None, seq, V_DIM), lambda bi, h, i: (bi, 0, h_blk + h)),
            pl.BlockSpec((None, seq, V_DIM), lambda bi, h, i: (bi, 0, 2 * h_blk + h)),
            pl.BlockSpec((META_PAD, V_DIM), lambda bi, h, i: (0, h_blk + h)),
            pl.BlockSpec((META_PAD, V_DIM), lambda bi, h, i: (0, 2 * h_blk + h)),
        ],
        out_specs=[
            pl.BlockSpec((None, t, V_DIM), lambda bi, h, i: (bi, i, h)),
            pl.BlockSpec((None, META_PAD, V_DIM), lambda bi, h, i: (bi, 0, h)),
        ],
        out_shape=[
            jax.ShapeDtypeStruct((b, seq, N_HEADS * V_DIM), BF16),
            jax.ShapeDtypeStruct((b, META_PAD, N_HEADS * V_DIM), BF16),
        ],
        scratch_shapes=[
            pltpu.VMEM((acc_rows, seq), BF16),
            pltpu.VMEM((acc_rows, META_PAD), BF16),
            pltpu.VMEM((3, t, t), F32),
            pltpu.VMEM((2, t, t), F32),
            pltpu.VMEM((2, t, t), F32),
        ],
        compiler_params=_params("parallel", "parallel", "arbitrary"),
        name="diff_attention",
    )(*[v.reshape(1, HEAD_DIM) for v in lams], subw.reshape(1, V_DIM),
      y, y_meta_pad, y, y, y_meta_pad, y_meta_pad)


def _chan_dft_kernel(f_ref, cs_ref, o_ref):
    t = jnp.dot(f_ref[...], cs_ref[...], preferred_element_type=F32)
    o_ref[0] = t[:, :GROUP_DIM].astype(o_ref.dtype)
    o_ref[1] = t[:, GROUP_DIM:].astype(o_ref.dtype)


def _chan_dft(y, cs, *, tr):
    b, rows, in_width = y.shape
    f_blk0 = in_width // GROUP_DIM - N_GROUPS
    return pl.pallas_call(
        _chan_dft_kernel,
        grid=(b, rows // tr, N_GROUPS),
        in_specs=[
            pl.BlockSpec((None, tr, GROUP_DIM), lambda bi, r, g: (bi, r, f_blk0 + g)),
            pl.BlockSpec((GROUP_DIM, 2 * GROUP_DIM), lambda bi, r, g: (0, 0)),
        ],
        out_specs=pl.BlockSpec((None, 2, tr, GROUP_DIM), lambda bi, r, g: (bi, 0, r, g)),
        out_shape=jax.ShapeDtypeStruct((b, 2, rows, N_GROUPS * GROUP_DIM), BF16),
        compiler_params=_params("parallel", "parallel", "parallel"),
        name="chan_dft",
    )(y, cs)


def _pos_dft_kernel(mr_ref, mm_ref, t_ref, tm_ref, nw_ref, o_ref, *, scale):
    r = jnp.dot(mr_ref[...], t_ref[...], preferred_element_type=F32)
    r = r + jnp.dot(mm_ref[...], tm_ref[...], preferred_element_type=F32)
    r = r * scale
    inv = lax.rsqrt(jnp.mean(r * r, axis=-1, keepdims=True) + NORM_EPS)
    o_ref[...] = (r * inv * nw_ref[...]).astype(o_ref.dtype)


def _pos_dft(mr, mm, t, t_meta, nw, *, tp, scale):
    p_rows = mr.shape[0]
    b, k_real, width = t.shape
    k_meta = t_meta.shape[1]
    return pl.pallas_call(
        functools.partial(_pos_dft_kernel, scale=scale),
        grid=(b, N_GROUPS, p_rows // tp),
        in_specs=[
            pl.BlockSpec((tp, k_real), lambda bi, g, p: (p, 0)),
            pl.BlockSpec((tp, k_meta), lambda bi, g, p: (p, 0)),
            pl.BlockSpec((None, k_real, GROUP_DIM), lambda bi, g, p: (bi, 0, g)),
            pl.BlockSpec((None, k_meta, GROUP_DIM), lambda bi, g, p: (0, 0, g)),
            pl.BlockSpec((1, GROUP_DIM), lambda bi, g, p: (0, g)),
        ],
        out_specs=pl.BlockSpec((None, tp, GROUP_DIM), lambda bi, g, p: (bi, p, g)),
        out_shape=jax.ShapeDtypeStruct((b, p_rows, width), BF16),
        compiler_params=_params("parallel", "parallel", "arbitrary"),
        name="pos_dft",
    )(mr, mm, t, t_meta, nw.reshape(1, width))


TABLE_STEP = 64


def _dft_tables(seq):
    l_total = N_META + seq
    c = jnp.arange(GROUP_DIM, dtype=jnp.int32)
    ang_c = ((c[:, None] * c[None, :]) % GROUP_DIM).astype(F32) * (2.0 * math.pi / GROUP_DIM)
    cs = jnp.concatenate([jnp.cos(ang_c), jnp.sin(ang_c)], axis=1).astype(BF16)

    pos = jnp.arange(l_total, dtype=jnp.int32)
    n_hi = -(-l_total // TABLE_STEP)

    to_rad = 2.0 * math.pi / l_total
    p_hi = TABLE_STEP * jnp.arange(n_hi, dtype=jnp.int32)
    p_lo = jnp.arange(TABLE_STEP, dtype=jnp.int32)
    hi = ((p_hi[:, None] * pos[None, :]) % l_total).astype(F32) * to_rad
    lo = ((p_lo[:, None] * pos[None, :]) % l_total).astype(F32) * to_rad
    c_hi, s_hi = jnp.cos(hi)[:, None, :], jnp.sin(hi)[:, None, :]
    c_lo, s_lo = jnp.cos(lo)[None, :, :], jnp.sin(lo)[None, :, :]
    cos_l = (c_hi * c_lo - s_hi * s_lo).reshape(n_hi * TABLE_STEP, l_total)[:l_total]
    sin_l = (s_hi * c_lo + c_hi * s_lo).reshape(n_hi * TABLE_STEP, l_total)[:l_total]
    real_cols = jnp.concatenate([cos_l[:, N_META:], -sin_l[:, N_META:]], axis=1).astype(BF16)
    meta_cols = jnp.concatenate([cos_l[:, :N_META], -sin_l[:, :N_META]], axis=1).astype(BF16)
    return cs, real_cols, meta_cols


def _matmul_res_kernel(*refs, n_pairs):
    res_ref, o_ref = refs[2 * n_pairs], refs[2 * n_pairs + 1]
    acc = res_ref[...]
    for k in range(n_pairs):
        acc = acc + jnp.dot(refs[2 * k][...], refs[2 * k + 1][...], preferred_element_type=F32)
    o_ref[...] = acc


def _matmul_res(pairs, res, *, tm, tn):
    m, n = res.shape
    in_specs, args = [], []
    for a, w, blk in pairs:
        k = a.shape[1]
        in_specs += [pl.BlockSpec((tm, k), lambda i, j: (i, 0)),
                     pl.BlockSpec((k, tn), lambda i, j, blk=blk: (blk, j))]
        args += [a, w]
    in_specs.append(pl.BlockSpec((tm, tn), lambda i, j: (i, j)))
    return pl.pallas_call(
        functools.partial(_matmul_res_kernel, n_pairs=len(pairs)),
        grid=(m // tm, n // tn),
        in_specs=in_specs,
        out_specs=pl.BlockSpec((tm, tn), lambda i, j: (i, j)),
        out_shape=jax.ShapeDtypeStruct((m, n), F32),
        compiler_params=_params("parallel", "arbitrary"),
        name="matmul_res",
    )(*args, res)


HALO_ROWS = 8
NORM_ROWS = 256


def _up_glu_kernel(x_ref, nw_ref, wg_ref, wv_ref, halo_ref, cw_ref, cb_ref, o_ref, hn_ref, *, tm):
    @pl.when(pl.program_id(1) == 0)
    def _():
        def norm_rows(r, carry):
            rows = pl.ds(pl.multiple_of(r * NORM_ROWS, NORM_ROWS), NORM_ROWS)
            x = x_ref[rows, :]
            inv = lax.rsqrt(jnp.mean(x * x, axis=-1, keepdims=True) + NORM_EPS)
            hn_ref[rows, :] = (x * inv * nw_ref[...]).astype(BF16)
            return carry
        lax.fori_loop(0, tm // NORM_ROWS, norm_rows, 0)

    hn = hn_ref[...]
    g = jnp.dot(hn, wg_ref[...], preferred_element_type=F32)
    v = jnp.dot(hn, wv_ref[...], preferred_element_type=F32)
    row = lax.broadcasted_iota(jnp.int32, g.shape, 0)
    g_prev = jnp.where(row == 0, halo_ref[0:1, :], pltpu.roll(g, 1, axis=0))
    g_next = jnp.where(row == tm - 1, halo_ref[1:2, :], pltpu.roll(g, tm - 1, axis=0))
    conv = g_prev * cw_ref[0:1, :] + g * cw_ref[1:2, :] + g_next * cw_ref[2:3, :] + cb_ref[...]
    gelu = 0.5 * conv * (1.0 + lax.erf(conv * math.sqrt(0.5)))
    o_ref[...] = (gelu * v).astype(o_ref.dtype)


def _up_glu(x, nw, w_up, halo, conv_w, conv_b, *, tm, tc):
    m, d = x.shape
    d_ff = w_up.shape[1] // 2
    n_c = d_ff // tc
    return pl.pallas_call(
        functools.partial(_up_glu_kernel, tm=tm),
        grid=(m // tm, n_c),
        in_specs=[
            pl.BlockSpec((tm, d), lambda i, c: (i, 0), pipeline_mode=pl.Buffered(1)),
            pl.BlockSpec((1, d), lambda i, c: (0, 0)),
            pl.BlockSpec((d, tc), lambda i, c: (0, c)),
            pl.BlockSpec((d, tc), lambda i, c: (0, n_c + c)),
            pl.BlockSpec((None, HALO_ROWS, tc), lambda i, c: (i, 0, c)),
            pl.BlockSpec((CONV_WIDTH, tc), lambda i, c: (0, c)),
            pl.BlockSpec((1, tc), lambda i, c: (0, c)),
        ],
        out_specs=pl.BlockSpec((tm, tc), lambda i, c: (i, c)),
        out_shape=jax.ShapeDtypeStruct((m, d_ff), BF16),
        scratch_shapes=[pltpu.VMEM((tm, d), BF16)],
        compiler_params=_params("parallel", "arbitrary"),
        name="up_glu",
    )(x, nw.reshape(1, d), w_up, w_up, halo, conv_w, conv_b.reshape(1, d_ff))


def _gate_halo(h_mid, h_mid_meta, nw, w_up, *, b, seq, tm, d_ff):
    d = h_mid.shape[-1]
    n_t = seq // tm
    h3 = h_mid.reshape(b, seq, d)
    last_rows = h3[:, tm - 1::tm][:, :n_t - 1]
    first_rows = h3[:, tm::tm]
    meta_last = h_mid_meta.reshape(b, N_META, d)[:, N_META - 1:]
    rows = jnp.concatenate([meta_last, last_rows, first_rows], axis=1).reshape(-1, d)
    n_rows = rows.shape[0]
    pad = -n_rows % 16
    rows = jnp.pad(rows, ((0, pad), (0, 0)))
    gates = _norm_matmul(rows, nw, w_up, tm=n_rows + pad, tn=256, n_cols=d_ff)[:n_rows]
    gates = gates.astype(F32).reshape(b, 2 * n_t - 1, d_ff)
    before = gates[:, :n_t]
    after = jnp.concatenate([gates[:, n_t:], jnp.zeros((b, 1, d_ff), F32)], axis=1)
    halo = jnp.stack([before, after], axis=2)
    halo = jnp.pad(halo, ((0, 0), (0, 0), (0, HALO_ROWS - 2), (0, 0)))
    return halo.reshape(b * n_t, HALO_ROWS, d_ff)


def _rmsnorm_kernel(x_ref, w_ref, o_ref):
    x = x_ref[...]
    inv = lax.rsqrt(jnp.mean(x * x, axis=-1, keepdims=True) + NORM_EPS)
    o_ref[...] = x * inv * w_ref[...]


def _rmsnorm(x, w, *, tm):
    m, d = x.shape
    return pl.pallas_call(
        _rmsnorm_kernel,
        grid=(m // tm,),
        in_specs=[pl.BlockSpec((tm, d), lambda i: (i, 0)), pl.BlockSpec((1, d), lambda i: (0, 0))],
        out_specs=pl.BlockSpec((tm, d), lambda i: (i, 0)),
        out_shape=jax.ShapeDtypeStruct((m, d), F32),
        compiler_params=_params("parallel"),
        name="final_rmsnorm",
    )(x, w.reshape(1, d))


def kernel(x, meta_tokens, norm1_w, w_in, lambda_q1, lambda_k1, lambda_q2, lambda_k2,
           diff_subln_w, fourier_norm_w, w_out, norm2_w, w_up, conv_w, conv_b, w_down,
           final_norm_w):
    b, seq, d = x.shape
    m = b * seq
    in_width = w_in.shape[-1]
    d_ff = w_down.shape[1]
    assert w_in.shape[0] == 1 and meta_tokens.shape[0] == N_META

    w_in_b = w_in[0].astype(BF16)
    w_out_b = w_out[0].astype(BF16)
    w_up_b = w_up[0].astype(BF16)
    w_down_b = w_down[0].astype(BF16)
    lams = (lambda_q1[0], lambda_k1[0], lambda_q2[0], lambda_k2[0])

    y = _norm_matmul(x.reshape(m, d), norm1_w[0], w_in_b, tm=512, tn=1024).reshape(b, seq, in_width)
    y_meta = _norm_matmul(meta_tokens, norm1_w[0], w_in_b, tm=N_META, tn=1024)

    y_meta_pad = jnp.pad(y_meta, ((0, META_PAD - N_META), (0, 0)))
    attn, attn_meta = _attention(lams, diff_subln_w[0], y, y_meta_pad, t=min(512, seq))
    attn_meta = attn_meta[:, :N_META]

    cs, real_cols, meta_cols = _dft_tables(seq)
    four_w = N_GROUPS * GROUP_DIM
    t = _chan_dft(y, cs, tr=min(1024, seq)).reshape(b, 2 * seq, four_w)
    t_meta = _chan_dft(y_meta[None], cs, tr=N_META).reshape(1, 2 * N_META, four_w)
    scale = 1.0 / math.sqrt((N_META + seq) * GROUP_DIM)
    four = _pos_dft(real_cols[N_META:], meta_cols[N_META:], t, t_meta, fourier_norm_w[0],
                    tp=512, scale=scale)
    four_meta = _pos_dft(real_cols[:N_META], meta_cols[:N_META], t, t_meta, fourier_norm_w[0],
                         tp=N_META, scale=scale)

    attn_w = attn.shape[-1]
    h_mid = _matmul_res([(attn.reshape(m, attn_w), w_out_b, 0), (four.reshape(m, four_w), w_out_b, 1)],
                        x.reshape(m, d), tm=512, tn=min(1024, d))
    meta_res = jnp.broadcast_to(meta_tokens[None], (b, N_META, d)).reshape(b * N_META, d)
    h_mid_meta = _matmul_res([(attn_meta.reshape(b * N_META, attn_w), w_out_b, 0),
                              (four_meta.reshape(b * N_META, four_w), w_out_b, 1)],
                             meta_res, tm=b * N_META, tn=min(1024, d))

    tm_up = min(1024, seq)
    halo = _gate_halo(h_mid, h_mid_meta, norm2_w[0], w_up_b, b=b, seq=seq, tm=tm_up, d_ff=d_ff)
    z = _up_glu(h_mid, norm2_w[0], w_up_b, halo, conv_w[0], conv_b[0], tm=tm_up, tc=256)
    h_out = _matmul_res([(z, w_down_b, 0)], h_mid, tm=512, tn=min(512, d))

    return _rmsnorm(h_out, final_norm_w, tm=512).reshape(b, seq, d)
```

```python
import functools
import math

import jax
import jax.numpy as jnp
from jax import lax
from jax.experimental import pallas as pl
from jax.experimental.pallas import tpu as pltpu

N_META = 16
N_HEADS = 16
HEAD_DIM = 64
V_DIM = 128
N_GROUPS = 4
GROUP_DIM = 512
CONV_WIDTH = 3
NORM_EPS = 1e-6
SUBLN_EPS = 1e-5
LAMBDA_INIT = 0.8 - 0.6 * math.exp(-0.3 * 0)

VMEM_LIMIT_BYTES = 56 * 1024 * 1024
BF16 = jnp.bfloat16
F32 = jnp.float32


def _params(*sem):
    return pltpu.CompilerParams(dimension_semantics=sem, vmem_limit_bytes=VMEM_LIMIT_BYTES)


def _norm_matmul_kernel(x_ref, nw_ref, w_ref, o_ref, hn_ref):
    @pl.when(pl.program_id(1) == 0)
    def _():
        x = x_ref[...]
        inv = lax.rsqrt(jnp.mean(x * x, axis=-1, keepdims=True) + NORM_EPS)
        hn_ref[...] = (x * inv * nw_ref[...]).astype(BF16)

    o_ref[...] = jnp.dot(hn_ref[...], w_ref[...], preferred_element_type=F32).astype(o_ref.dtype)


def _norm_matmul(x, nw, w, *, tm, tn, n_cols=None):
    m, d = x.shape
    n = w.shape[1] if n_cols is None else n_cols
    return pl.pallas_call(
        _norm_matmul_kernel,
        grid=(m // tm, n // tn),
        in_specs=[
            pl.BlockSpec((tm, d), lambda i, j: (i, 0)),
            pl.BlockSpec((1, d), lambda i, j: (0, 0)),
            pl.BlockSpec((d, tn), lambda i, j: (0, j)),
        ],
        out_specs=pl.BlockSpec((tm, tn), lambda i, j: (i, j)),
        out_shape=jax.ShapeDtypeStruct((m, n), BF16),
        scratch_shapes=[pltpu.VMEM((tm, d), BF16)],
        compiler_params=_params("parallel", "arbitrary"),
        name="norm_matmul",
    )(x, nw.reshape(1, d), w)


META_PAD = 128
SUM_ROWS = 16
MASKED = -1e30
LOG2E = 1.0 / math.log(2.0)
BEFORE, DIAG, AFTER = 0, 1, 2
SKIP_BITS = 160.0
NORM_SLACK = 1.01


def _attn_kernel(lq1_ref, lk1_ref, lq2_ref, lk2_ref, subw_ref, q_ref, qm_ref, k_ref, v_ref,
                 km_ref, vm_ref, o_ref, om_ref, vt_ref, vtm_ref, bias_ref, ta_ref, tb_ref,
                 ma_ref, mb_ref, m_ref, acc_ref, knorm_ref, *, t):
    h = pl.program_id(1)
    qi = pl.program_id(2)
    seq = k_ref.shape[0]
    n_chunks = seq // t
    assert n_chunks >= 2, "every query tile needs a neighbouring key chunk"
    acc_rows = V_DIM + SUM_ROWS

    slope = LOG2E * jnp.exp2(jnp.full((1, 1), -8.0 / N_HEADS, F32) * jnp.asarray(h + 1, F32))
    lane = lax.broadcasted_iota(jnp.int32, (1, V_DIM), 1)
    map_cols = (lax.broadcasted_iota(jnp.int32, (V_DIM, V_DIM), 0) // HEAD_DIM
                == lax.broadcasted_iota(jnp.int32, (V_DIM, V_DIM), 1)).astype(BF16)
    map_rows = (lax.broadcasted_iota(jnp.int32, (8, V_DIM), 1) // HEAD_DIM
                == lax.broadcasted_iota(jnp.int32, (8, V_DIM), 0)).astype(BF16)

    @pl.when(qi == 0)
    def _():
        for c in range(n_chunks):
            vt_ref[0:V_DIM, c * t:(c + 1) * t] = v_ref[c * t:(c + 1) * t, :].astype(F32).T.astype(BF16)
        vt_ref[V_DIM:acc_rows, :] = jnp.ones((SUM_ROWS, seq), BF16)
        vtm_ref[0:V_DIM, :] = vm_ref[...].astype(F32).T.astype(BF16)
        vtm_ref[V_DIM:acc_rows, :] = jnp.ones((SUM_ROWS, META_PAD), BF16)
        row = lax.broadcasted_iota(jnp.int32, (t, t), 0)
        col = lax.broadcasted_iota(jnp.int32, (t, t), 1)
        bias_ref[BEFORE] = slope * row.astype(F32)
        bias_ref[DIAG] = -slope * jnp.abs(col - row).astype(F32)
        bias_ref[AFTER] = -slope * row.astype(F32)
        k_sq = jnp.square(k_ref[...].astype(F32)).astype(BF16)
        sq_norms = jnp.dot(k_sq, map_cols, preferred_element_type=F32)
        knorm_ref[0:1, :] = jnp.sqrt(jnp.max(sq_norms, axis=0, keepdims=True))

    lam = (jnp.exp(jnp.sum(lq1_ref[...] * lk1_ref[...], axis=-1, keepdims=True))
           - jnp.exp(jnp.sum(lq2_ref[...] * lk2_ref[...], axis=-1, keepdims=True))
           + LAMBDA_INIT)
    meta_row = lax.broadcasted_iota(jnp.int32, (META_PAD, 1), 0)
    meta_valid = meta_row < N_META

    def attend(q, q_pos, tile_idx, out_ref):
        nq = q.shape[0]
        q32 = q.astype(F32) * (LOG2E / math.sqrt(HEAD_DIM))
        qs = (jnp.where(lane < HEAD_DIM, q32, 0.0).astype(BF16),
              jnp.where(lane >= HEAD_DIM, q32, 0.0).astype(BF16))

        def scores(k_c):
            return tuple(lax.dot_general(k_c, qc, (((1,), (1,)), ((), ())), preferred_element_type=F32)
                         for qc in qs)

        def update(c, tt, tt_max, row_bias, vt_c):
            m = m_ref[c, 0:1, 0:nq]
            m_new = jnp.maximum(m, tt_max + row_bias)
            p = jnp.exp2(tt - (m_new - row_bias))
            acc_ref[c, :, 0:nq] = (jnp.exp2(m - m_new) * acc_ref[c, :, 0:nq]
                                   + jnp.dot(vt_c, p.astype(BF16), preferred_element_type=F32))
            m_ref[c, 0:1, 0:nq] = m_new

        def produce(c_idx, kind, buf_ref, max_ref):
            k_c = k_ref[pl.ds(pl.multiple_of(c_idx * t, t), t), :]
            key_bias = bias_ref[kind, :, 0:nq]
            for c, s in enumerate(scores(k_c)):
                tt = s + key_bias
                buf_ref[c, :, 0:nq] = tt
                max_ref[c, 0:1, 0:nq] = jnp.max(tt, axis=0, keepdims=True)

        def consume(c_idx, kind, buf_ref, max_ref):
            sign = 1.0 - kind
            k_pos0 = jnp.asarray(N_META + c_idx * t, F32)
            row_bias = (-sign) * slope * (q_pos - k_pos0)
            vt_c = vt_ref[:, pl.ds(pl.multiple_of(c_idx * t, t), t)]
            for c in range(2):
                update(c, buf_ref[c, :, 0:nq], max_ref[c, 0:1, 0:nq], row_bias, vt_c)

        m_ref[...] = jnp.full(m_ref.shape, -jnp.inf, F32)
        acc_ref[...] = jnp.zeros(acc_ref.shape, F32)
        buf_a, buf_b = (ta_ref, ma_ref), (tb_ref, mb_ref)
        if tile_idx is None:
            first = (jnp.int32(0), AFTER)
        else:
            produce(tile_idx, DIAG, *buf_a)
            has_before = tile_idx > 0
            first = (jnp.where(has_before, tile_idx - 1, tile_idx + 1),
                     jnp.where(has_before, BEFORE, AFTER))

        if tile_idx is None:
            key_bias, row_bias = -slope * jnp.abs(q_pos - meta_row.astype(F32)), jnp.zeros((1, nq), F32)
        else:
            key_bias, row_bias = bias_ref[BEFORE, 0:META_PAD, 0:nq], -slope * q_pos
        for c, s in enumerate(scores(km_ref[...])):
            tt = jnp.where(meta_valid, s + key_bias, MASKED)
            update(c, tt, jnp.max(tt, axis=0, keepdims=True), row_bias, vtm_ref[...])

        produce(*first, *buf_b)
        if tile_idx is not None:
            consume(tile_idx, DIAG, *buf_a)

        q_sq = jnp.square(q32.astype(BF16).astype(F32)).astype(BF16)
        q_norm = jnp.sqrt(lax.dot_general(map_rows, q_sq, (((1,), (1,)), ((), ())),
                                          preferred_element_type=F32))
        bound = jnp.full((1, 1), -jnp.inf, F32)
        for c in range(2):
            gap = knorm_ref[0:1, c:c + 1] * q_norm[c:c + 1] * NORM_SLACK - m_ref[c, 0:1, 0:nq]
            bound = jnp.maximum(bound, jnp.max(gap, axis=-1, keepdims=True))
        reach = ((bound + SKIP_BITS) / slope - 1.0) / t + 1.0
        n_side = jnp.max(jnp.clip(jnp.floor(reach), 1.0, float(n_chunks))).astype(jnp.int32)
        if tile_idx is None:
            n_before, n_todo = 0, jnp.minimum(n_chunks, n_side)
        else:
            n_before = jnp.minimum(tile_idx, n_side)
            n_todo = n_before + jnp.minimum(n_chunks - 1 - tile_idx, n_side)

        def todo(k):
            if tile_idx is None:
                return k, AFTER
            before = k < n_before
            c_idx = jnp.where(before, tile_idx - 1 - k, tile_idx + 1 + k - n_before)
            return c_idx, jnp.where(before, BEFORE, AFTER)

        def body(k, carry):
            nxt = jnp.minimum(k + 1, n_todo - 1)
            for parity, (cur, nxt_buf) in enumerate(((buf_b, buf_a), (buf_a, buf_b))):
                @pl.when(k % 2 == parity)
                def _():
                    produce(*todo(nxt), *nxt_buf)
                    consume(*todo(k), *cur)
            return carry

        lax.fori_loop(0, n_todo, body, 0)

        acc1, acc2 = acc_ref[0, :, 0:nq], acc_ref[1, :, 0:nq]
        a = acc1[0:V_DIM] / acc1[V_DIM:V_DIM + 1] - lam * (acc2[0:V_DIM] / acc2[V_DIM:V_DIM + 1])
        inv = lax.rsqrt(jnp.mean(a * a, axis=0, keepdims=True) + SUBLN_EPS)
        out_ref[...] = ((a * inv).T * subw_ref[...] * (1.0 - LAMBDA_INIT)).astype(out_ref.dtype)

    q_pos = (N_META + qi * t + lax.broadcasted_iota(jnp.int32, (1, t), 1)).astype(F32)
    attend(q_ref[...], q_pos, qi, o_ref)

    @pl.when(qi == 0)
    def _():
        pos = jnp.minimum(lax.broadcasted_iota(jnp.int32, (1, META_PAD), 1), N_META - 1)
        attend(qm_ref[...], pos.astype(F32), None, om_ref)


def _attention(lams, subw, y, y_meta_pad, *, t):
    b, seq, _ = y.shape
    h_blk = N_HEADS
    vec = lambda n: pl.BlockSpec((1, n), lambda bi, h, i: (0, 0))
    acc_rows = V_DIM + SUM_ROWS
    return pl.pallas_call(
        functools.partial(_attn_kernel, t=t),
        grid=(b, N_HEADS, seq // t),
        in_specs=[
            vec(HEAD_DIM), vec(HEAD_DIM), vec(HEAD_DIM), vec(HEAD_DIM), vec(V_DIM),
            pl.BlockSpec((None, t, V_DIM), lambda bi, h, i: (bi, i, h)),
            pl.BlockSpec((META_PAD, V_DIM), lambda bi, h, i: (0, h)),
            pl.BlockSpec((None, seq, V_DIM), lambda bi, h, i: (bi, 0, h_blk + h)),
            pl.BlockSpec((None, seq, V_DIM), lambda bi, h, i: (bi, 0, 2 * h_blk + h)),
            pl.BlockSpec((META_PAD, V_DIM), lambda bi, h, i: (0, h_blk + h)),
            pl.BlockSpec((META_PAD, V_DIM), lambda bi, h, i: (0, 2 * h_blk + h)),
        ],
        out_specs=[
            pl.BlockSpec((None, t, V_DIM), lambda bi, h, i: (bi, i, h)),
            pl.BlockSpec((None, META_PAD, V_DIM), lambda bi, h, i: (bi, 0, h)),
        ],
        out_shape=[
            jax.ShapeDtypeStruct((b, seq, N_HEADS * V_DIM), BF16),
            jax.ShapeDtypeStruct((b, META_PAD, N_HEADS * V_DIM), BF16),
        ],
        scratch_shapes=[
            pltpu.VMEM((acc_rows, seq), BF16),
            pltpu.VMEM((acc_rows, META_PAD), BF16),
            pltpu.VMEM((3, t, t), F32),
            pltpu.VMEM((2, t, t), F32),
            pltpu.VMEM((2, t, t), F32),
            pltpu.VMEM((2, 8, t), F32),
            pltpu.VMEM((2, 8, t), F32),
            pltpu.VMEM((2, 8, t), F32),
            pltpu.VMEM((2, acc_rows, t), F32),
            pltpu.VMEM((8, V_DIM), F32),
        ],
        compiler_params=_params("parallel", "parallel", "arbitrary"),
        name="diff_attention",
    )(*[v.reshape(1, HEAD_DIM) for v in lams], subw.reshape(1, V_DIM),
      y, y_meta_pad, y, y, y_meta_pad, y_meta_pad)


def _chan_dft_kernel(f_ref, cs_ref, o_ref):
    t = jnp.dot(f_ref[...], cs_ref[...], preferred_element_type=F32)
    o_ref[0] = t[:, :GROUP_DIM].astype(o_ref.dtype)
    o_ref[1] = t[:, GROUP_DIM:].astype(o_ref.dtype)


def _chan_dft(y, cs, *, tr):
    b, rows, in_width = y.shape
    f_blk0 = in_width // GROUP_DIM - N_GROUPS
    return pl.pallas_call(
        _chan_dft_kernel,
        grid=(b, rows // tr, N_GROUPS),
        in_specs=[
            pl.BlockSpec((None, tr, GROUP_DIM), lambda bi, r, g: (bi, r, f_blk0 + g)),
            pl.BlockSpec((GROUP_DIM, 2 * GROUP_DIM), lambda bi, r, g: (0, 0)),
        ],
        out_specs=pl.BlockSpec((None, 2, tr, GROUP_DIM), lambda bi, r, g: (bi, 0, r, g)),
        out_shape=jax.ShapeDtypeStruct((b, 2, rows, N_GROUPS * GROUP_DIM), BF16),
        compiler_params=_params("parallel", "parallel", "parallel"),
        name="chan_dft",
    )(y, cs)


def _pos_dft_kernel(mr_ref, mm_ref, t_ref, tm_ref, nw_ref, o_ref, *, scale):
    r = jnp.dot(mr_ref[...], t_ref[...], preferred_element_type=F32)
    r = r + jnp.dot(mm_ref[...], tm_ref[...], preferred_element_type=F32)
    r = r * scale
    inv = lax.rsqrt(jnp.mean(r * r, axis=-1, keepdims=True) + NORM_EPS)
    o_ref[...] = (r * inv * nw_ref[...]).astype(o_ref.dtype)


def _pos_dft(mr, mm, t, t_meta, nw, *, tp, scale):
    p_rows = mr.shape[0]
    b, k_real, width = t.shape
    k_meta = t_meta.shape[1]
    return pl.pallas_call(
        functools.partial(_pos_dft_kernel, scale=scale),
        grid=(b, N_GROUPS, p_rows // tp),
        in_specs=[
            pl.BlockSpec((tp, k_real), lambda bi, g, p: (p, 0)),
            pl.BlockSpec((tp, k_meta), lambda bi, g, p: (p, 0)),
            pl.BlockSpec((None, k_real, GROUP_DIM), lambda bi, g, p: (bi, 0, g)),
            pl.BlockSpec((None, k_meta, GROUP_DIM), lambda bi, g, p: (0, 0, g)),
            pl.BlockSpec((1, GROUP_DIM), lambda bi, g, p: (0, g)),
        ],
        out_specs=pl.BlockSpec((None, tp, GROUP_DIM), lambda bi, g, p: (bi, p, g)),
        out_shape=jax.ShapeDtypeStruct((b, p_rows, width), BF16),
        compiler_params=_params("parallel", "parallel", "arbitrary"),
        name="pos_dft",
    )(mr, mm, t, t_meta, nw.reshape(1, width))


TABLE_STEP = 64


def _dft_tables(seq):
    l_total = N_META + seq
    c = jnp.arange(GROUP_DIM, dtype=jnp.int32)
    ang_c = ((c[:, None] * c[None, :]) % GROUP_DIM).astype(F32) * (2.0 * math.pi / GROUP_DIM)
    cs = jnp.concatenate([jnp.cos(ang_c), jnp.sin(ang_c)], axis=1).astype(BF16)

    pos = jnp.arange(l_total, dtype=jnp.int32)
    n_hi = -(-l_total // TABLE_STEP)

    to_rad = 2.0 * math.pi / l_total
    p_hi = TABLE_STEP * jnp.arange(n_hi, dtype=jnp.int32)
    p_lo = jnp.arange(TABLE_STEP, dtype=jnp.int32)
    hi = ((p_hi[:, None] * pos[None, :]) % l_total).astype(F32) * to_rad
    lo = ((p_lo[:, None] * pos[None, :]) % l_total).astype(F32) * to_rad
    c_hi, s_hi = jnp.cos(hi)[:, None, :], jnp.sin(hi)[:, None, :]
    c_lo, s_lo = jnp.cos(lo)[None, :, :], jnp.sin(lo)[None, :, :]
    cos_l = (c_hi * c_lo - s_hi * s_lo).reshape(n_hi * TABLE_STEP, l_total)[:l_total]
    sin_l = (s_hi * c_lo + c_hi * s_lo).reshape(n_hi * TABLE_STEP, l_total)[:l_total]
    real_cols = jnp.concatenate([cos_l[:, N_META:], -sin_l[:, N_META:]], axis=1).astype(BF16)
    meta_cols = jnp.concatenate([cos_l[:, :N_META], -sin_l[:, :N_META]], axis=1).astype(BF16)
    return cs, real_cols, meta_cols


def _matmul_res_kernel(*refs, n_pairs):
    res_ref, o_ref = refs[2 * n_pairs], refs[2 * n_pairs + 1]
    acc = res_ref[...]
    for k in range(n_pairs):
        acc = acc + jnp.dot(refs[2 * k][...], refs[2 * k + 1][...], preferred_element_type=F32)
    o_ref[...] = acc


def _matmul_res(pairs, res, *, tm, tn):
    m, n = res.shape
    in_specs, args = [], []
    for a, w, blk in pairs:
        k = a.shape[1]
        in_specs += [pl.BlockSpec((tm, k), lambda i, j: (i, 0)),
                     pl.BlockSpec((k, tn), lambda i, j, blk=blk: (blk, j))]
        args += [a, w]
    in_specs.append(pl.BlockSpec((tm, tn), lambda i, j: (i, j)))
    return pl.pallas_call(
        functools.partial(_matmul_res_kernel, n_pairs=len(pairs)),
        grid=(m // tm, n // tn),
        in_specs=in_specs,
        out_specs=pl.BlockSpec((tm, tn), lambda i, j: (i, j)),
        out_shape=jax.ShapeDtypeStruct((m, n), F32),
        compiler_params=_params("parallel", "arbitrary"),
        name="matmul_res",
    )(*args, res)


HALO_ROWS = 8
NORM_ROWS = 256


def _up_glu_kernel(x_ref, nw_ref, wg_ref, wv_ref, halo_ref, cw_ref, cb_ref, o_ref, hn_ref, *, tm):
    @pl.when(pl.program_id(1) == 0)
    def _():
        def norm_rows(r, carry):
            rows = pl.ds(pl.multiple_of(r * NORM_ROWS, NORM_ROWS), NORM_ROWS)
            x = x_ref[rows, :]
            inv = lax.rsqrt(jnp.mean(x * x, axis=-1, keepdims=True) + NORM_EPS)
            hn_ref[rows, :] = (x * inv * nw_ref[...]).astype(BF16)
            return carry
        lax.fori_loop(0, tm // NORM_ROWS, norm_rows, 0)

    hn = hn_ref[...]
    g = jnp.dot(hn, wg_ref[...], preferred_element_type=F32)
    v = jnp.dot(hn, wv_ref[...], preferred_element_type=F32)
    row = lax.broadcasted_iota(jnp.int32, g.shape, 0)
    g_prev = jnp.where(row == 0, halo_ref[0:1, :], pltpu.roll(g, 1, axis=0))
    g_next = jnp.where(row == tm - 1, halo_ref[1:2, :], pltpu.roll(g, tm - 1, axis=0))
    conv = g_prev * cw_ref[0:1, :] + g * cw_ref[1:2, :] + g_next * cw_ref[2:3, :] + cb_ref[...]
    gelu = 0.5 * conv * (1.0 + lax.erf(conv * math.sqrt(0.5)))
    o_ref[...] = (gelu * v).astype(o_ref.dtype)


def _up_glu(x, nw, w_up, halo, conv_w, conv_b, *, tm, tc):
    m, d = x.shape
    d_ff = w_up.shape[1] // 2
    n_c = d_ff // tc
    return pl.pallas_call(
        functools.partial(_up_glu_kernel, tm=tm),
        grid=(m // tm, n_c),
        in_specs=[
            pl.BlockSpec((tm, d), lambda i, c: (i, 0), pipeline_mode=pl.Buffered(1)),
            pl.BlockSpec((1, d), lambda i, c: (0, 0)),
            pl.BlockSpec((d, tc), lambda i, c: (0, c)),
            pl.BlockSpec((d, tc), lambda i, c: (0, n_c + c)),
            pl.BlockSpec((None, HALO_ROWS, tc), lambda i, c: (i, 0, c)),
            pl.BlockSpec((CONV_WIDTH, tc), lambda i, c: (0, c)),
            pl.BlockSpec((1, tc), lambda i, c: (0, c)),
        ],
        out_specs=pl.BlockSpec((tm, tc), lambda i, c: (i, c)),
        out_shape=jax.ShapeDtypeStruct((m, d_ff), BF16),
        scratch_shapes=[pltpu.VMEM((tm, d), BF16)],
        compiler_params=_params("parallel", "arbitrary"),
        name="up_glu",
    )(x, nw.reshape(1, d), w_up, w_up, halo, conv_w, conv_b.reshape(1, d_ff))


def _gate_halo(h_mid, h_mid_meta, nw, w_up, *, b, seq, tm, d_ff):
    d = h_mid.shape[-1]
    n_t = seq // tm
    h3 = h_mid.reshape(b, seq, d)
    last_rows = h3[:, tm - 1::tm][:, :n_t - 1]
    first_rows = h3[:, tm::tm]
    meta_last = h_mid_meta.reshape(b, N_META, d)[:, N_META - 1:]
    rows = jnp.concatenate([meta_last, last_rows, first_rows], axis=1).reshape(-1, d)
    n_rows = rows.shape[0]
    pad = -n_rows % 16
    rows = jnp.pad(rows, ((0, pad), (0, 0)))
    gates = _norm_matmul(rows, nw, w_up, tm=n_rows + pad, tn=256, n_cols=d_ff)[:n_rows]
    gates = gates.astype(F32).reshape(b, 2 * n_t - 1, d_ff)
    before = gates[:, :n_t]
    after = jnp.concatenate([gates[:, n_t:], jnp.zeros((b, 1, d_ff), F32)], axis=1)
    halo = jnp.stack([before, after], axis=2)
    halo = jnp.pad(halo, ((0, 0), (0, 0), (0, HALO_ROWS - 2), (0, 0)))
    return halo.reshape(b * n_t, HALO_ROWS, d_ff)


def _rmsnorm_kernel(x_ref, w_ref, o_ref):
    x = x_ref[...]
    inv = lax.rsqrt(jnp.mean(x * x, axis=-1, keepdims=True) + NORM_EPS)
    o_ref[...] = x * inv * w_ref[...]


def _rmsnorm(x, w, *, tm):
    m, d = x.shape
    return pl.pallas_call(
        _rmsnorm_kernel,
        grid=(m // tm,),
        in_specs=[pl.BlockSpec((tm, d), lambda i: (i, 0)), pl.BlockSpec((1, d), lambda i: (0, 0))],
        out_specs=pl.BlockSpec((tm, d), lambda i: (i, 0)),
        out_shape=jax.ShapeDtypeStruct((m, d), F32),
        compiler_params=_params("parallel"),
        name="final_rmsnorm",
    )(x, w.reshape(1, d))


def kernel(x, meta_tokens, norm1_w, w_in, lambda_q1, lambda_k1, lambda_q2, lambda_k2,
           diff_subln_w, fourier_norm_w, w_out, norm2_w, w_up, conv_w, conv_b, w_down,
           final_norm_w):
    b, seq, d = x.shape
    m = b * seq
    in_width = w_in.shape[-1]
    d_ff = w_down.shape[1]
    assert w_in.shape[0] == 1 and meta_tokens.shape[0] == N_META

    w_in_b = w_in[0].astype(BF16)
    w_out_b = w_out[0].astype(BF16)
    w_up_b = w_up[0].astype(BF16)
    w_down_b = w_down[0].astype(BF16)
    lams = (lambda_q1[0], lambda_k1[0], lambda_q2[0], lambda_k2[0])

    y = _norm_matmul(x.reshape(m, d), norm1_w[0], w_in_b, tm=512, tn=1024).reshape(b, seq, in_width)
    y_meta = _norm_matmul(meta_tokens, norm1_w[0], w_in_b, tm=N_META, tn=1024)

    y_meta_pad = jnp.pad(y_meta, ((0, META_PAD - N_META), (0, 0)))
    attn, attn_meta = _attention(lams, diff_subln_w[0], y, y_meta_pad, t=min(512, seq))
    attn_meta = attn_meta[:, :N_META]

    cs, real_cols, meta_cols = _dft_tables(seq)
    four_w = N_GROUPS * GROUP_DIM
    t = _chan_dft(y, cs, tr=min(1024, seq)).reshape(b, 2 * seq, four_w)
    t_meta = _chan_dft(y_meta[None], cs, tr=N_META).reshape(1, 2 * N_META, four_w)
    scale = 1.0 / math.sqrt((N_META + seq) * GROUP_DIM)
    four = _pos_dft(real_cols[N_META:], meta_cols[N_META:], t, t_meta, fourier_norm_w[0],
                    tp=512, scale=scale)
    four_meta = _pos_dft(real_cols[:N_META], meta_cols[:N_META], t, t_meta, fourier_norm_w[0],
                         tp=N_META, scale=scale)

    attn_w = attn.shape[-1]
    h_mid = _matmul_res([(attn.reshape(m, attn_w), w_out_b, 0), (four.reshape(m, four_w), w_out_b, 1)],
                        x.reshape(m, d), tm=512, tn=min(1024, d))
    meta_res = jnp.broadcast_to(meta_tokens[None], (b, N_META, d)).reshape(b * N_META, d)
    h_mid_meta = _matmul_res([(attn_meta.reshape(b * N_META, attn_w), w_out_b, 0),
                              (four_meta.reshape(b * N_META, four_w), w_out_b, 1)],
                             meta_res, tm=b * N_META, tn=min(1024, d))

    tm_up = min(1024, seq)
    halo = _gate_halo(h_mid, h_mid_meta, norm2_w[0], w_up_b, b=b, seq=seq, tm=tm_up, d_ff=d_ff)
    z = _up_glu(h_mid, norm2_w[0], w_up_b, halo, conv_w[0], conv_b[0], tm=tm_up, tc=256)
    h_out = _matmul_res([(z, w_down_b, 0)], h_mid, tm=512, tn=min(512, d))

    return _rmsnorm(h_out, final_norm_w, tm=512).reshape(b, seq, d)
```

```python
import functools
import math

import jax
import jax.numpy as jnp
from jax import lax
from jax.experimental import pallas as pl
from jax.experimental.pallas import tpu as pltpu

N_META = 16
N_HEADS = 16
HEAD_DIM = 64
V_DIM = 128
N_GROUPS = 4
GROUP_DIM = 512
CONV_WIDTH = 3
NORM_EPS = 1e-6
SUBLN_EPS = 1e-5
LAMBDA_INIT = 0.8 - 0.6 * math.exp(-0.3 * 0)

VMEM_LIMIT_BYTES = 56 * 1024 * 1024
BF16 = jnp.bfloat16
F32 = jnp.float32


def _params(*sem):
    return pltpu.CompilerParams(dimension_semantics=sem, vmem_limit_bytes=VMEM_LIMIT_BYTES)


def _norm_rows(x_ref, nw_ref, rows):
    x = x_ref[rows, :]
    inv = lax.rsqrt(jnp.mean(x * x, axis=-1, keepdims=True) + NORM_EPS)
    return (x * inv * nw_ref[...]).astype(BF16)


def _norm_matmul_kernel(x_ref, nw_ref, w_ref, o_ref, hn_ref, *, row_chunks):
    @pl.when(pl.program_id(1) == 0)
    def _():
        for rows in row_chunks:
            hn = _norm_rows(x_ref, nw_ref, rows)
            hn_ref[rows, :] = hn
            o_ref[rows, :] = jnp.dot(hn, w_ref[...], preferred_element_type=F32).astype(o_ref.dtype)

    @pl.when(pl.program_id(1) != 0)
    def _():
        o_ref[...] = jnp.dot(hn_ref[...], w_ref[...], preferred_element_type=F32).astype(o_ref.dtype)


def _norm_matmul(x, nw, w, *, tm, tn, n_cols=None):
    m, d = x.shape
    n = w.shape[1] if n_cols is None else n_cols
    step = min(tm, NORM_ROWS)
    row_chunks = tuple(slice(r, r + step) for r in range(0, tm, step))
    return pl.pallas_call(
        functools.partial(_norm_matmul_kernel, row_chunks=row_chunks),
        grid=(m // tm, n // tn),
        in_specs=[
            pl.BlockSpec((tm, d), lambda i, j: (i, 0)),
            pl.BlockSpec((1, d), lambda i, j: (0, 0)),
            pl.BlockSpec((d, tn), lambda i, j: (0, j)),
        ],
        out_specs=pl.BlockSpec((tm, tn), lambda i, j: (i, j)),
        out_shape=jax.ShapeDtypeStruct((m, n), BF16),
        scratch_shapes=[pltpu.VMEM((tm, d), BF16)],
        compiler_params=_params("parallel", "arbitrary"),
        name="norm_matmul",
    )(x, nw.reshape(1, d), w)


META_PAD = 128
SUM_ROWS = 16
MASKED = -1e30
LOG2E = 1.0 / math.log(2.0)
BEFORE, DIAG, AFTER = 0, 1, 2
SKIP_BITS = 160.0
NORM_SLACK = 1.01


def _attn_kernel(lq1_ref, lk1_ref, lq2_ref, lk2_ref, subw_ref, q_ref, qm_ref, k_ref, v_ref,
                 km_ref, vm_ref, o_ref, om_ref, vt_ref, vtm_ref, bias_ref, ta_ref, tb_ref,
                 ma_ref, mb_ref, m_ref, acc_ref, knorm_ref, *, t):
    h = pl.program_id(1)
    qi = pl.program_id(2)
    seq = k_ref.shape[0]
    n_chunks = seq // t
    assert n_chunks >= 2, "every query tile needs a neighbouring key chunk"
    acc_rows = V_DIM + SUM_ROWS

    slope = LOG2E * jnp.exp2(jnp.full((1, 1), -8.0 / N_HEADS, F32) * jnp.asarray(h + 1, F32))
    lane = lax.broadcasted_iota(jnp.int32, (1, V_DIM), 1)
    map_cols = (lax.broadcasted_iota(jnp.int32, (V_DIM, V_DIM), 0) // HEAD_DIM
                == lax.broadcasted_iota(jnp.int32, (V_DIM, V_DIM), 1)).astype(BF16)
    map_rows = (lax.broadcasted_iota(jnp.int32, (8, V_DIM), 1) // HEAD_DIM
                == lax.broadcasted_iota(jnp.int32, (8, V_DIM), 0)).astype(BF16)

    @pl.when(qi == 0)
    def _():
        for c in range(n_chunks):
            vt_ref[0:V_DIM, c * t:(c + 1) * t] = v_ref[c * t:(c + 1) * t, :].astype(F32).T.astype(BF16)
        vt_ref[V_DIM:acc_rows, :] = jnp.ones((SUM_ROWS, seq), BF16)
        vtm_ref[0:V_DIM, :] = vm_ref[...].astype(F32).T.astype(BF16)
        vtm_ref[V_DIM:acc_rows, :] = jnp.ones((SUM_ROWS, META_PAD), BF16)
        row = lax.broadcasted_iota(jnp.int32, (t, t), 0)
        col = lax.broadcasted_iota(jnp.int32, (t, t), 1)
        bias_ref[BEFORE] = slope * row.astype(F32)
        bias_ref[DIAG] = -slope * jnp.abs(col - row).astype(F32)
        bias_ref[AFTER] = -slope * row.astype(F32)
        k_sq = jnp.square(k_ref[...].astype(F32)).astype(BF16)
        sq_norms = jnp.dot(k_sq, map_cols, preferred_element_type=F32)
        knorm_ref[0:1, :] = jnp.sqrt(jnp.max(sq_norms, axis=0, keepdims=True))

    lam = (jnp.exp(jnp.sum(lq1_ref[...] * lk1_ref[...], axis=-1, keepdims=True))
           - jnp.exp(jnp.sum(lq2_ref[...] * lk2_ref[...], axis=-1, keepdims=True))
           + LAMBDA_INIT)
    meta_row = lax.broadcasted_iota(jnp.int32, (META_PAD, 1), 0)
    meta_valid = meta_row < N_META

    def attend(q, q_pos, tile_idx, out_ref):
        nq = q.shape[0]
        q32 = q.astype(F32) * (LOG2E / math.sqrt(HEAD_DIM))
        qs = (jnp.where(lane < HEAD_DIM, q32, 0.0).astype(BF16),
              jnp.where(lane >= HEAD_DIM, q32, 0.0).astype(BF16))

        def scores(k_c):
            return tuple(lax.dot_general(k_c, qc, (((1,), (1,)), ((), ())), preferred_element_type=F32)
                         for qc in qs)

        def update(c, tt, tt_max, row_bias, vt_c):
            m = m_ref[c, 0:1, 0:nq]
            m_new = jnp.maximum(m, tt_max + row_bias)
            p = jnp.exp2(tt - (m_new - row_bias))
            acc_ref[c, :, 0:nq] = (jnp.exp2(m - m_new) * acc_ref[c, :, 0:nq]
                                   + jnp.dot(vt_c, p.astype(BF16), preferred_element_type=F32))
            m_ref[c, 0:1, 0:nq] = m_new

        def produce(c_idx, kind, buf_ref, max_ref):
            k_c = k_ref[pl.ds(pl.multiple_of(c_idx * t, t), t), :]
            key_bias = bias_ref[kind, :, 0:nq]
            for c, s in enumerate(scores(k_c)):
                tt = s + key_bias
                buf_ref[c, :, 0:nq] = tt
                max_ref[c, 0:1, 0:nq] = jnp.max(tt, axis=0, keepdims=True)

        def consume(c_idx, kind, buf_ref, max_ref):
            sign = 1.0 - kind
            k_pos0 = jnp.asarray(N_META + c_idx * t, F32)
            row_bias = (-sign) * slope * (q_pos - k_pos0)
            vt_c = vt_ref[:, pl.ds(pl.multiple_of(c_idx * t, t), t)]
            for c in range(2):
                update(c, buf_ref[c, :, 0:nq], max_ref[c, 0:1, 0:nq], row_bias, vt_c)

        m_ref[...] = jnp.full(m_ref.shape, -jnp.inf, F32)
        acc_ref[...] = jnp.zeros(acc_ref.shape, F32)
        buf_a, buf_b = (ta_ref, ma_ref), (tb_ref, mb_ref)
        if tile_idx is None:
            first = (jnp.int32(0), AFTER)
        else:
            produce(tile_idx, DIAG, *buf_a)
            has_before = tile_idx > 0
            first = (jnp.where(has_before, tile_idx - 1, tile_idx + 1),
                     jnp.where(has_before, BEFORE, AFTER))

        if tile_idx is None:
            key_bias, row_bias = -slope * jnp.abs(q_pos - meta_row.astype(F32)), jnp.zeros((1, nq), F32)
        else:
            key_bias, row_bias = bias_ref[BEFORE, 0:META_PAD, 0:nq], -slope * q_pos
        for c, s in enumerate(scores(km_ref[...])):
            tt = jnp.where(meta_valid, s + key_bias, MASKED)
            update(c, tt, jnp.max(tt, axis=0, keepdims=True), row_bias, vtm_ref[...])

        produce(*first, *buf_b)
        if tile_idx is not None:
            consume(tile_idx, DIAG, *buf_a)

        q_sq = jnp.square(q32.astype(BF16).astype(F32)).astype(BF16)
        q_norm = jnp.sqrt(lax.dot_general(map_rows, q_sq, (((1,), (1,)), ((), ())),
                                          preferred_element_type=F32))
        bound = jnp.full((1, 1), -jnp.inf, F32)
        for c in range(2):
            gap = knorm_ref[0:1, c:c + 1] * q_norm[c:c + 1] * NORM_SLACK - m_ref[c, 0:1, 0:nq]
            bound = jnp.maximum(bound, jnp.max(gap, axis=-1, keepdims=True))
        reach = ((bound + SKIP_BITS) / slope - 1.0) / t + 1.0
        n_side = jnp.max(jnp.clip(jnp.floor(reach), 1.0, float(n_chunks))).astype(jnp.int32)
        if tile_idx is None:
            n_before, n_todo = 0, jnp.minimum(n_chunks, n_side)
        else:
            n_before = jnp.minimum(tile_idx, n_side)
            n_todo = n_before + jnp.minimum(n_chunks - 1 - tile_idx, n_side)

        def todo(k):
            if tile_idx is None:
                return k, AFTER
            before = k < n_before
            c_idx = jnp.where(before, tile_idx - 1 - k, tile_idx + 1 + k - n_before)
            return c_idx, jnp.where(before, BEFORE, AFTER)

        def body(k, carry):
            nxt = jnp.minimum(k + 1, n_todo - 1)
            for parity, (cur, nxt_buf) in enumerate(((buf_b, buf_a), (buf_a, buf_b))):
                @pl.when(k % 2 == parity)
                def _():
                    produce(*todo(nxt), *nxt_buf)
                    consume(*todo(k), *cur)
            return carry

        lax.fori_loop(0, n_todo, body, 0)

        acc1, acc2 = acc_ref[0, :, 0:nq], acc_ref[1, :, 0:nq]
        a = acc1[0:V_DIM] / acc1[V_DIM:V_DIM + 1] - lam * (acc2[0:V_DIM] / acc2[V_DIM:V_DIM + 1])
        inv = lax.rsqrt(jnp.mean(a * a, axis=0, keepdims=True) + SUBLN_EPS)
        out_ref[...] = ((a * inv).T * subw_ref[...] * (1.0 - LAMBDA_INIT)).astype(out_ref.dtype)

    q_pos = (N_META + qi * t + lax.broadcasted_iota(jnp.int32, (1, t), 1)).astype(F32)
    attend(q_ref[...], q_pos, qi, o_ref)

    @pl.when(qi == 0)
    def _():
        pos = jnp.minimum(lax.broadcasted_iota(jnp.int32, (1, META_PAD), 1), N_META - 1)
        attend(qm_ref[...], pos.astype(F32), None, om_ref)


def _attention(lams, subw, y, y_meta_pad, *, t):
    b, seq, _ = y.shape
    h_blk = N_HEADS
    vec = lambda n: pl.BlockSpec((1, n), lambda bi, h, i: (0, 0))
    acc_rows = V_DIM + SUM_ROWS
    return pl.pallas_call(
        functools.partial(_attn_kernel, t=t),
        grid=(b, N_HEADS, seq // t),
        in_specs=[
            vec(HEAD_DIM), vec(HEAD_DIM), vec(HEAD_DIM), vec(HEAD_DIM), vec(V_DIM),
            pl.BlockSpec((None, t, V_DIM), lambda bi, h, i: (bi, i, h)),
            pl.BlockSpec((META_PAD, V_DIM), lambda bi, h, i: (0, h)),
            pl.BlockSpec((None, seq, V_DIM), lambda bi, h, i: (bi, 0, h_blk + h)),
            pl.BlockSpec((None, seq, V_DIM), lambda bi, h, i: (bi, 0, 2 * h_blk + h)),
            pl.BlockSpec((META_PAD, V_DIM), lambda bi, h, i: (0, h_blk + h)),
            pl.BlockSpec((META_PAD, V_DIM), lambda bi, h, i: (0, 2 * h_blk + h)),
        ],
        out_specs=[
            pl.BlockSpec((None, t, V_DIM), lambda bi, h, i: (bi, i, h)),
            pl.BlockSpec((None, META_PAD, V_DIM), lambda bi, h, i: (bi, 0, h)),
        ],
        out_shape=[
            jax.ShapeDtypeStruct((b, seq, N_HEADS * V_DIM), BF16),
            jax.ShapeDtypeStruct((b, META_PAD, N_HEADS * V_DIM), BF16),
        ],
        scratch_shapes=[
            pltpu.VMEM((acc_rows, seq), BF16),
            pltpu.VMEM((acc_rows, META_PAD), BF16),
            pltpu.VMEM((3, t, t), F32),
            pltpu.VMEM((2, t, t), F32),
            pltpu.VMEM((2, t, t), F32),
            pltpu.VMEM((2, 8, t), F32),
            pltpu.VMEM((2, 8, t), F32),
            pltpu.VMEM((2, 8, t), F32),
            pltpu.VMEM((2, acc_rows, t), F32),
            pltpu.VMEM((8, V_DIM), F32),
        ],
        compiler_params=_params("parallel", "parallel", "arbitrary"),
        name="diff_attention",
    )(*[v.reshape(1, HEAD_DIM) for v in lams], subw.reshape(1, V_DIM),
      y, y_meta_pad, y, y, y_meta_pad, y_meta_pad)


def _chan_dft_kernel(f_ref, cs_ref, o_ref):
    t = jnp.dot(f_ref[...], cs_ref[...], preferred_element_type=F32)
    o_ref[0] = t[:, :GROUP_DIM].astype(o_ref.dtype)
    o_ref[1] = t[:, GROUP_DIM:].astype(o_ref.dtype)


def _chan_dft(y, cs, *, tr):
    b, rows, in_width = y.shape
    f_blk0 = in_width // GROUP_DIM - N_GROUPS
    return pl.pallas_call(
        _chan_dft_kernel,
        grid=(b, rows // tr, N_GROUPS),
        in_specs=[
            pl.BlockSpec((None, tr, GROUP_DIM), lambda bi, r, g: (bi, r, f_blk0 + g)),
            pl.BlockSpec((GROUP_DIM, 2 * GROUP_DIM), lambda bi, r, g: (0, 0)),
        ],
        out_specs=pl.BlockSpec((None, 2, tr, GROUP_DIM), lambda bi, r, g: (bi, 0, r, g)),
        out_shape=jax.ShapeDtypeStruct((b, 2, rows, N_GROUPS * GROUP_DIM), BF16),
        compiler_params=_params("parallel", "parallel", "parallel"),
        name="chan_dft",
    )(y, cs)


def _pos_dft_kernel(mr_ref, mm_ref, t_ref, tm_ref, nw_ref, o_ref, *, scale):
    r = jnp.dot(mr_ref[...], t_ref[...], preferred_element_type=F32)
    r = r + jnp.dot(mm_ref[...], tm_ref[...], preferred_element_type=F32)
    r = r * scale
    inv = lax.rsqrt(jnp.mean(r * r, axis=-1, keepdims=True) + NORM_EPS)
    o_ref[...] = (r * inv * nw_ref[...]).astype(o_ref.dtype)


def _pos_dft(mr, mm, t, t_meta, nw, *, tp, scale):
    p_rows = mr.shape[0]
    b, k_real, width = t.shape
    k_meta = t_meta.shape[1]
    return pl.pallas_call(
        functools.partial(_pos_dft_kernel, scale=scale),
        grid=(b, N_GROUPS, p_rows // tp),
        in_specs=[
            pl.BlockSpec((tp, k_real), lambda bi, g, p: (p, 0)),
            pl.BlockSpec((tp, k_meta), lambda bi, g, p: (p, 0)),
            pl.BlockSpec((None, k_real, GROUP_DIM), lambda bi, g, p: (bi, 0, g)),
            pl.BlockSpec((None, k_meta, GROUP_DIM), lambda bi, g, p: (0, 0, g)),
            pl.BlockSpec((1, GROUP_DIM), lambda bi, g, p: (0, g)),
        ],
        out_specs=pl.BlockSpec((None, tp, GROUP_DIM), lambda bi, g, p: (bi, p, g)),
        out_shape=jax.ShapeDtypeStruct((b, p_rows, width), BF16),
        compiler_params=_params("parallel", "parallel", "arbitrary"),
        name="pos_dft",
    )(mr, mm, t, t_meta, nw.reshape(1, width))


TABLE_STEP = 64


def _dft_tables(seq):
    l_total = N_META + seq
    c = jnp.arange(GROUP_DIM, dtype=jnp.int32)
    ang_c = ((c[:, None] * c[None, :]) % GROUP_DIM).astype(F32) * (2.0 * math.pi / GROUP_DIM)
    cs = jnp.concatenate([jnp.cos(ang_c), jnp.sin(ang_c)], axis=1).astype(BF16)

    to_rad = 2.0 * math.pi / l_total
    half_pi = jnp.asarray([0.0, 0.5 * math.pi], F32)
    meta_p = jnp.arange(N_META, dtype=jnp.int32)
    real_p = N_META + jnp.arange(seq, dtype=jnp.int32)
    real_l, real_phase = jnp.tile(real_p, 2), jnp.repeat(half_pi, seq)
    meta_l, meta_phase = jnp.tile(meta_p, 2), jnp.repeat(half_pi, N_META)

    def angle(p, l, phase):
        return ((p[:, None] * l[None, :]) % l_total).astype(F32) * to_rad + phase[None, :]

    def direct(p, l, phase):
        return jnp.cos(angle(p, l, phase)).astype(BF16)

    hi = angle(N_META + TABLE_STEP * jnp.arange(seq // TABLE_STEP, dtype=jnp.int32), real_l, 0.0 * real_phase)
    lo = angle(jnp.arange(TABLE_STEP, dtype=jnp.int32), real_l, real_phase)
    real_rows_real_cols = (jnp.cos(hi)[:, None, :] * jnp.cos(lo)[None, :, :]
                           - jnp.sin(hi)[:, None, :] * jnp.sin(lo)[None, :, :]
                           ).reshape(seq, 2 * seq).astype(BF16)
    real_rows = (real_rows_real_cols, direct(real_p, meta_l, meta_phase))
    meta_rows = (direct(meta_p, real_l, real_phase), direct(meta_p, meta_l, meta_phase))
    return cs, real_rows, meta_rows


def _matmul_res_kernel(*refs, n_pairs):
    res_ref, o_ref = refs[2 * n_pairs], refs[2 * n_pairs + 1]
    acc = res_ref[...]
    for k in range(n_pairs):
        acc = acc + jnp.dot(refs[2 * k][...], refs[2 * k + 1][...], preferred_element_type=F32)
    o_ref[...] = acc


def _matmul_res(pairs, res, *, tm, tn):
    m, n = res.shape
    in_specs, args = [], []
    for a, w, blk in pairs:
        k = a.shape[1]
        in_specs += [pl.BlockSpec((tm, k), lambda i, j: (i, 0)),
                     pl.BlockSpec((k, tn), lambda i, j, blk=blk: (blk, j))]
        args += [a, w]
    in_specs.append(pl.BlockSpec((tm, tn), lambda i, j: (i, j)))
    return pl.pallas_call(
        functools.partial(_matmul_res_kernel, n_pairs=len(pairs)),
        grid=(m // tm, n // tn),
        in_specs=in_specs,
        out_specs=pl.BlockSpec((tm, tn), lambda i, j: (i, j)),
        out_shape=jax.ShapeDtypeStruct((m, n), F32),
        compiler_params=_params("parallel", "arbitrary"),
        name="matmul_res",
    )(*args, res)


HALO_ROWS = 8
NORM_ROWS = 256


def _up_glu_kernel(x_ref, nw_ref, wg_ref, wv_ref, halo_ref, cw_ref, cb_ref, o_ref, hn_ref,
                   gv0_ref, gv1_ref, *, tm, n_c):
    c = pl.program_id(1)
    step = min(tm, NORM_ROWS)
    row_chunks = tuple(slice(r, r + step) for r in range(0, tm, step))

    def matmul(gv_ref, rows, hn):
        gv_ref[0, rows, :] = jnp.dot(hn, wg_ref[...], preferred_element_type=F32)
        gv_ref[1, rows, :] = jnp.dot(hn, wv_ref[...], preferred_element_type=F32)

    def finish(gv_ref, rows):
        g, v = gv_ref[0, rows, :], gv_ref[1, rows, :]
        prev_row = halo_ref[0:1, :] if rows.start == 0 else gv_ref[0, rows.start - 1:rows.start, :]
        next_row = halo_ref[1:2, :] if rows.stop == tm else gv_ref[0, rows.stop:rows.stop + 1, :]
        row = lax.broadcasted_iota(jnp.int32, g.shape, 0)
        g_prev = jnp.where(row == 0, prev_row, pltpu.roll(g, 1, axis=0))
        g_next = jnp.where(row == step - 1, next_row, pltpu.roll(g, step - 1, axis=0))
        conv = g_prev * cw_ref[0:1, :] + g * cw_ref[1:2, :] + g_next * cw_ref[2:3, :] + cb_ref[...]
        gelu = 0.5 * conv * (1.0 + lax.erf(conv * math.sqrt(0.5)))
        o_ref[rows, :] = (gelu * v).astype(o_ref.dtype)

    @pl.when(c == 0)
    def _():
        for rows in row_chunks:
            hn = _norm_rows(x_ref, nw_ref, rows)
            hn_ref[rows, :] = hn
            matmul(gv0_ref, rows, hn)

    for parity, (cur_ref, prev_ref) in enumerate(((gv0_ref, gv1_ref), (gv1_ref, gv0_ref))):
        @pl.when((c > 0) & (c < n_c) & (c % 2 == parity))
        def _():
            for rows in row_chunks:
                finish(prev_ref, rows)
                matmul(cur_ref, rows, hn_ref[rows, :])

    @pl.when(c == n_c)
    def _():
        for rows in row_chunks:
            finish(gv1_ref if n_c % 2 == 0 else gv0_ref, rows)


def _up_glu(x, nw, w_up, halo, conv_w, conv_b, *, tm, tc):
    m, d = x.shape
    d_ff = w_up.shape[1] // 2
    n_c = d_ff // tc
    w_col = lambda c: jnp.minimum(c, n_c - 1)
    done = lambda c: jnp.maximum(c - 1, 0)
    return pl.pallas_call(
        functools.partial(_up_glu_kernel, tm=tm, n_c=n_c),
        grid=(m // tm, n_c + 1),
        in_specs=[
            pl.BlockSpec((tm, d), lambda i, c: (i, 0), pipeline_mode=pl.Buffered(1)),
            pl.BlockSpec((1, d), lambda i, c: (0, 0)),
            pl.BlockSpec((d, tc), lambda i, c: (0, w_col(c))),
            pl.BlockSpec((d, tc), lambda i, c: (0, n_c + w_col(c))),
            pl.BlockSpec((None, HALO_ROWS, tc), lambda i, c: (i, 0, done(c))),
            pl.BlockSpec((CONV_WIDTH, tc), lambda i, c: (0, done(c))),
            pl.BlockSpec((1, tc), lambda i, c: (0, done(c))),
        ],
        out_specs=pl.BlockSpec((tm, tc), lambda i, c: (i, done(c))),
        out_shape=jax.ShapeDtypeStruct((m, d_ff), BF16),
        scratch_shapes=[pltpu.VMEM((tm, d), BF16),
                        pltpu.VMEM((2, tm, tc), F32), pltpu.VMEM((2, tm, tc), F32)],
        compiler_params=_params("parallel", "arbitrary"),
        name="up_glu",
    )(x, nw.reshape(1, d), w_up, w_up, halo, conv_w, conv_b.reshape(1, d_ff))


def _gate_halo(h_mid, h_mid_meta, nw, w_up, *, b, seq, tm, d_ff):
    d = h_mid.shape[-1]
    n_t = seq // tm
    h3 = h_mid.reshape(b, seq, d)
    last_rows = h3[:, tm - 1::tm][:, :n_t - 1]
    first_rows = h3[:, tm::tm]
    meta_last = h_mid_meta.reshape(b, N_META, d)[:, N_META - 1:]
    rows = jnp.concatenate([meta_last, last_rows, first_rows], axis=1).reshape(-1, d)
    n_rows = rows.shape[0]
    pad = -n_rows % 16
    rows = jnp.pad(rows, ((0, pad), (0, 0)))
    gates = _norm_matmul(rows, nw, w_up, tm=n_rows + pad, tn=256, n_cols=d_ff)[:n_rows]
    gates = gates.astype(F32).reshape(b, 2 * n_t - 1, d_ff)
    before = gates[:, :n_t]
    after = jnp.concatenate([gates[:, n_t:], jnp.zeros((b, 1, d_ff), F32)], axis=1)
    halo = jnp.stack([before, after], axis=2)
    halo = jnp.pad(halo, ((0, 0), (0, 0), (0, HALO_ROWS - 2), (0, 0)))
    return halo.reshape(b * n_t, HALO_ROWS, d_ff)


def _rmsnorm_kernel(x_ref, w_ref, o_ref):
    x = x_ref[...]
    inv = lax.rsqrt(jnp.mean(x * x, axis=-1, keepdims=True) + NORM_EPS)
    o_ref[...] = x * inv * w_ref[...]


def _rmsnorm(x, w, *, tm):
    m, d = x.shape
    return pl.pallas_call(
        _rmsnorm_kernel,
        grid=(m // tm,),
        in_specs=[pl.BlockSpec((tm, d), lambda i: (i, 0)), pl.BlockSpec((1, d), lambda i: (0, 0))],
        out_specs=pl.BlockSpec((tm, d), lambda i: (i, 0)),
        out_shape=jax.ShapeDtypeStruct((m, d), F32),
        compiler_params=_params("parallel"),
        name="final_rmsnorm",
    )(x, w.reshape(1, d))


def kernel(x, meta_tokens, norm1_w, w_in, lambda_q1, lambda_k1, lambda_q2, lambda_k2,
           diff_subln_w, fourier_norm_w, w_out, norm2_w, w_up, conv_w, conv_b, w_down,
           final_norm_w):
    b, seq, d = x.shape
    m = b * seq
    in_width = w_in.shape[-1]
    d_ff = w_down.shape[1]
    assert w_in.shape[0] == 1 and meta_tokens.shape[0] == N_META

    w_in_b = w_in[0].astype(BF16)
    w_out_b = w_out[0].astype(BF16)
    w_up_b = w_up[0].astype(BF16)
    w_down_b = w_down[0].astype(BF16)
    lams = (lambda_q1[0], lambda_k1[0], lambda_q2[0], lambda_k2[0])

    y = _norm_matmul(x.reshape(m, d), norm1_w[0], w_in_b, tm=512, tn=1024).reshape(b, seq, in_width)
    y_meta = _norm_matmul(meta_tokens, norm1_w[0], w_in_b, tm=N_META, tn=1024)

    y_meta_pad = jnp.pad(y_meta, ((0, META_PAD - N_META), (0, 0)))
    attn, attn_meta = _attention(lams, diff_subln_w[0], y, y_meta_pad, t=min(512, seq))
    attn_meta = attn_meta[:, :N_META]

    cs, real_rows, meta_rows = _dft_tables(seq)
    four_w = N_GROUPS * GROUP_DIM
    t = _chan_dft(y, cs, tr=min(1024, seq)).reshape(b, 2 * seq, four_w)
    t_meta = _chan_dft(y_meta[None], cs, tr=N_META).reshape(1, 2 * N_META, four_w)
    scale = 1.0 / math.sqrt((N_META + seq) * GROUP_DIM)
    four = _pos_dft(*real_rows, t, t_meta, fourier_norm_w[0], tp=512, scale=scale)
    four_meta = _pos_dft(*meta_rows, t, t_meta, fourier_norm_w[0], tp=N_META, scale=scale)

    attn_w = attn.shape[-1]
    h_mid = _matmul_res([(attn.reshape(m, attn_w), w_out_b, 0), (four.reshape(m, four_w), w_out_b, 1)],
                        x.reshape(m, d), tm=512, tn=min(1024, d))
    meta_res = jnp.broadcast_to(meta_tokens[None], (b, N_META, d)).reshape(b * N_META, d)
    h_mid_meta = _matmul_res([(attn_meta.reshape(b * N_META, attn_w), w_out_b, 0),
                              (four_meta.reshape(b * N_META, four_w), w_out_b, 1)],
                             meta_res, tm=b * N_META, tn=min(1024, d))

    tm_up = min(1024, seq)
    halo = _gate_halo(h_mid, h_mid_meta, norm2_w[0], w_up_b, b=b, seq=seq, tm=tm_up, d_ff=d_ff)
    z = _up_glu(h_mid, norm2_w[0], w_up_b, halo, conv_w[0], conv_b[0], tm=tm_up, tc=256)
    h_out = _matmul_res([(z, w_down_b, 0)], h_mid, tm=512, tn=min(512, d))

    return _rmsnorm(h_out, final_norm_w, tm=512).reshape(b, seq, d)
```

```python
import functools
import math

import jax
import jax.numpy as jnp
from jax import lax
from jax.experimental import pallas as pl
from jax.experimental.pallas import tpu as pltpu

N_META = 16
N_HEADS = 16
HEAD_DIM = 64
V_DIM = 128
LANES = 128
N_GROUPS = 4
GROUP_DIM = 512
CONV_WIDTH = 3
NORM_EPS = 1e-6
SUBLN_EPS = 1e-5
LAMBDA_INIT = 0.8 - 0.6 * math.exp(-0.3 * 0)

VMEM_LIMIT_BYTES = 56 * 1024 * 1024
BF16 = jnp.bfloat16
F32 = jnp.float32


def _params(*sem):
    return pltpu.CompilerParams(dimension_semantics=sem, vmem_limit_bytes=VMEM_LIMIT_BYTES)


def _norm_rows(x_ref, nw_ref, rows):
    x = x_ref[rows, :]
    inv = lax.rsqrt(jnp.mean(x * x, axis=-1, keepdims=True) + NORM_EPS)
    return (x * inv * nw_ref[...]).astype(BF16)


def _norm_matmul_kernel(x_ref, nw_ref, w_ref, o_ref, hn_ref, *, row_chunks):
    @pl.when(pl.program_id(1) == 0)
    def _():
        for rows in row_chunks:
            hn = _norm_rows(x_ref, nw_ref, rows)
            hn_ref[rows, :] = hn
            o_ref[rows, :] = jnp.dot(hn, w_ref[...], preferred_element_type=F32).astype(o_ref.dtype)

    @pl.when(pl.program_id(1) != 0)
    def _():
        o_ref[...] = jnp.dot(hn_ref[...], w_ref[...], preferred_element_type=F32).astype(o_ref.dtype)


def _norm_matmul(x, nw, w, *, tm, tn, n_cols=None):
    m, d = x.shape
    n = w.shape[1] if n_cols is None else n_cols
    step = min(tm, NORM_ROWS)
    row_chunks = tuple(slice(r, r + step) for r in range(0, tm, step))
    return pl.pallas_call(
        functools.partial(_norm_matmul_kernel, row_chunks=row_chunks),
        grid=(m // tm, n // tn),
        in_specs=[
            pl.BlockSpec((tm, d), lambda i, j: (i, 0)),
            pl.BlockSpec((1, d), lambda i, j: (0, 0)),
            pl.BlockSpec((d, tn), lambda i, j: (0, j)),
        ],
        out_specs=pl.BlockSpec((tm, tn), lambda i, j: (i, j)),
        out_shape=jax.ShapeDtypeStruct((m, n), BF16),
        scratch_shapes=[pltpu.VMEM((tm, d), BF16)],
        compiler_params=_params("parallel", "arbitrary"),
        name="norm_matmul",
    )(x, nw.reshape(1, d), w)


META_PAD = 128
SUM_ROWS = 16
MASKED = -1e30
LOG2E = 1.0 / math.log(2.0)
BEFORE, DIAG, AFTER = 0, 1, 2
SKIP_BITS = 160.0
NORM_SLACK = 1.01


def _attn_kernel(lq1_ref, lk1_ref, lq2_ref, lk2_ref, subw_ref, q_ref, qm_ref, k_ref, v_ref,
                 km_ref, vm_ref, o_ref, om_ref, vt_ref, vtm_ref, bias_ref, ta_ref, tb_ref,
                 ma_ref, mb_ref, m_ref, acc_ref, knorm_ref, *, t):
    h = pl.program_id(1)
    qi = pl.program_id(2)
    seq = k_ref.shape[0]
    n_chunks = seq // t
    assert n_chunks >= 2, "every query tile needs a neighbouring key chunk"
    acc_rows = V_DIM + SUM_ROWS

    slope = LOG2E * jnp.exp2(jnp.full((1, 1), -8.0 / N_HEADS, F32) * jnp.asarray(h + 1, F32))
    lane = lax.broadcasted_iota(jnp.int32, (1, V_DIM), 1)
    map_cols = (lax.broadcasted_iota(jnp.int32, (V_DIM, V_DIM), 0) // HEAD_DIM
                == lax.broadcasted_iota(jnp.int32, (V_DIM, V_DIM), 1)).astype(BF16)
    map_rows = (lax.broadcasted_iota(jnp.int32, (8, V_DIM), 1) // HEAD_DIM
                == lax.broadcasted_iota(jnp.int32, (8, V_DIM), 0)).astype(BF16)

    @pl.when(qi == 0)
    def _():
        for c in range(n_chunks):
            vt_ref[0:V_DIM, c * t:(c + 1) * t] = v_ref[c * t:(c + 1) * t, :].astype(F32).T.astype(BF16)
        vt_ref[V_DIM:acc_rows, :] = jnp.ones((SUM_ROWS, seq), BF16)
        vtm_ref[0:V_DIM, :] = vm_ref[...].astype(F32).T.astype(BF16)
        vtm_ref[V_DIM:acc_rows, :] = jnp.ones((SUM_ROWS, META_PAD), BF16)
        row = lax.broadcasted_iota(jnp.int32, (t, t), 0)
        col = lax.broadcasted_iota(jnp.int32, (t, t), 1)
        bias_ref[BEFORE] = slope * row.astype(F32)
        bias_ref[DIAG] = -slope * jnp.abs(col - row).astype(F32)
        bias_ref[AFTER] = -slope * row.astype(F32)
        k_sq = jnp.square(k_ref[...].astype(F32)).astype(BF16)
        sq_norms = jnp.dot(k_sq, map_cols, preferred_element_type=F32)
        knorm_ref[0:1, :] = jnp.sqrt(jnp.max(sq_norms, axis=0, keepdims=True))

    lam = (jnp.exp(jnp.sum(lq1_ref[...] * lk1_ref[...], axis=-1, keepdims=True))
           - jnp.exp(jnp.sum(lq2_ref[...] * lk2_ref[...], axis=-1, keepdims=True))
           + LAMBDA_INIT)
    meta_row = lax.broadcasted_iota(jnp.int32, (META_PAD, 1), 0)
    meta_valid = meta_row < N_META

    def attend(q, q_pos, tile_idx, out_ref):
        nq = q.shape[0]
        q32 = q.astype(F32) * (LOG2E / math.sqrt(HEAD_DIM))
        qs = (jnp.where(lane < HEAD_DIM, q32, 0.0).astype(BF16),
              jnp.where(lane >= HEAD_DIM, q32, 0.0).astype(BF16))

        def scores(k_c):
            return tuple(lax.dot_general(k_c, qc, (((1,), (1,)), ((), ())), preferred_element_type=F32)
                         for qc in qs)

        def update(c, tt, tt_max, row_bias, vt_c):
            m = m_ref[c, 0:1, 0:nq]
            m_new = jnp.maximum(m, tt_max + row_bias)
            p = jnp.exp2(tt - (m_new - row_bias))
            acc_ref[c, :, 0:nq] = (jnp.exp2(m - m_new) * acc_ref[c, :, 0:nq]
                                   + jnp.dot(vt_c, p.astype(BF16), preferred_element_type=F32))
            m_ref[c, 0:1, 0:nq] = m_new

        def produce(c_idx, kind, buf_ref, max_ref):
            k_c = k_ref[pl.ds(pl.multiple_of(c_idx * t, t), t), :]
            key_bias = bias_ref[kind, :, 0:nq]
            for c, s in enumerate(scores(k_c)):
                tt = s + key_bias
                buf_ref[c, :, 0:nq] = tt
                max_ref[c, 0:1, 0:nq] = jnp.max(tt, axis=0, keepdims=True)

        def consume(c_idx, kind, buf_ref, max_ref):
            sign = 1.0 - kind
            k_pos0 = jnp.asarray(N_META + c_idx * t, F32)
            row_bias = (-sign) * slope * (q_pos - k_pos0)
            vt_c = vt_ref[:, pl.ds(pl.multiple_of(c_idx * t, t), t)]
            for c in range(2):
                update(c, buf_ref[c, :, 0:nq], max_ref[c, 0:1, 0:nq], row_bias, vt_c)

        m_ref[...] = jnp.full(m_ref.shape, -jnp.inf, F32)
        acc_ref[...] = jnp.zeros(acc_ref.shape, F32)
        buf_a, buf_b = (ta_ref, ma_ref), (tb_ref, mb_ref)
        if tile_idx is None:
            first = (jnp.int32(0), AFTER)
        else:
            produce(tile_idx, DIAG, *buf_a)
            has_before = tile_idx > 0
            first = (jnp.where(has_before, tile_idx - 1, tile_idx + 1),
                     jnp.where(has_before, BEFORE, AFTER))

        if tile_idx is None:
            key_bias, row_bias = -slope * jnp.abs(q_pos - meta_row.astype(F32)), jnp.zeros((1, nq), F32)
        else:
            key_bias, row_bias = bias_ref[BEFORE, 0:META_PAD, 0:nq], -slope * q_pos
        for c, s in enumerate(scores(km_ref[...])):
            tt = jnp.where(meta_valid, s + key_bias, MASKED)
            update(c, tt, jnp.max(tt, axis=0, keepdims=True), row_bias, vtm_ref[...])

        produce(*first, *buf_b)
        if tile_idx is not None:
            consume(tile_idx, DIAG, *buf_a)

        q_sq = jnp.square(q32.astype(BF16).astype(F32)).astype(BF16)
        q_norm = jnp.sqrt(lax.dot_general(map_rows, q_sq, (((1,), (1,)), ((), ())),
                                          preferred_element_type=F32))
        bound = jnp.full((1, 1), -jnp.inf, F32)
        for c in range(2):
            gap = knorm_ref[0:1, c:c + 1] * q_norm[c:c + 1] * NORM_SLACK - m_ref[c, 0:1, 0:nq]
            bound = jnp.maximum(bound, jnp.max(gap, axis=-1, keepdims=True))
        reach = ((bound + SKIP_BITS) / slope - 1.0) / t + 1.0
        n_side = jnp.max(jnp.clip(jnp.floor(reach), 1.0, float(n_chunks))).astype(jnp.int32)
        if tile_idx is None:
            n_before, n_todo = 0, jnp.minimum(n_chunks, n_side)
        else:
            n_before = jnp.minimum(tile_idx, n_side)
            n_todo = n_before + jnp.minimum(n_chunks - 1 - tile_idx, n_side)

        def todo(k):
            if tile_idx is None:
                return k, AFTER
            before = k < n_before
            c_idx = jnp.where(before, tile_idx - 1 - k, tile_idx + 1 + k - n_before)
            return c_idx, jnp.where(before, BEFORE, AFTER)

        def body(k, carry):
            nxt = jnp.minimum(k + 1, n_todo - 1)
            for parity, (cur, nxt_buf) in enumerate(((buf_b, buf_a), (buf_a, buf_b))):
                @pl.when(k % 2 == parity)
                def _():
                    produce(*todo(nxt), *nxt_buf)
                    consume(*todo(k), *cur)
            return carry

        lax.fori_loop(0, n_todo, body, 0)

        acc1, acc2 = acc_ref[0, :, 0:nq], acc_ref[1, :, 0:nq]
        a = acc1[0:V_DIM] / acc1[V_DIM:V_DIM + 1] - lam * (acc2[0:V_DIM] / acc2[V_DIM:V_DIM + 1])
        inv = lax.rsqrt(jnp.mean(a * a, axis=0, keepdims=True) + SUBLN_EPS)
        out_ref[...] = ((a * inv).T * subw_ref[...] * (1.0 - LAMBDA_INIT)).astype(out_ref.dtype)

    q_pos = (N_META + qi * t + lax.broadcasted_iota(jnp.int32, (1, t), 1)).astype(F32)
    attend(q_ref[...], q_pos, qi, o_ref)

    @pl.when(qi == 0)
    def _():
        pos = jnp.minimum(lax.broadcasted_iota(jnp.int32, (1, META_PAD), 1), N_META - 1)
        attend(qm_ref[...], pos.astype(F32), None, om_ref)


def _attention(lams, subw, y, y_meta_pad, *, t):
    b, seq, _ = y.shape
    h_blk = N_HEADS
    vec = lambda n: pl.BlockSpec((1, n), lambda bi, h, i: (0, 0))
    acc_rows = V_DIM + SUM_ROWS
    return pl.pallas_call(
        functools.partial(_attn_kernel, t=t),
        grid=(b, N_HEADS, seq // t),
        in_specs=[
            vec(HEAD_DIM), vec(HEAD_DIM), vec(HEAD_DIM), vec(HEAD_DIM), vec(V_DIM),
            pl.BlockSpec((None, t, V_DIM), lambda bi, h, i: (bi, i, h)),
            pl.BlockSpec((META_PAD, V_DIM), lambda bi, h, i: (0, h)),
            pl.BlockSpec((None, seq, V_DIM), lambda bi, h, i: (bi, 0, h_blk + h)),
            pl.BlockSpec((None, seq, V_DIM), lambda bi, h, i: (bi, 0, 2 * h_blk + h)),
            pl.BlockSpec((META_PAD, V_DIM), lambda bi, h, i: (0, h_blk + h)),
            pl.BlockSpec((META_PAD, V_DIM), lambda bi, h, i: (0, 2 * h_blk + h)),
        ],
        out_specs=[
            pl.BlockSpec((None, t, V_DIM), lambda bi, h, i: (bi, i, h)),
            pl.BlockSpec((None, META_PAD, V_DIM), lambda bi, h, i: (bi, 0, h)),
        ],
        out_shape=[
            jax.ShapeDtypeStruct((b, seq, N_HEADS * V_DIM), BF16),
            jax.ShapeDtypeStruct((b, META_PAD, N_HEADS * V_DIM), BF16),
        ],
        scratch_shapes=[
            pltpu.VMEM((acc_rows, seq), BF16),
            pltpu.VMEM((acc_rows, META_PAD), BF16),
            pltpu.VMEM((3, t, t), F32),
            pltpu.VMEM((2, t, t), F32),
            pltpu.VMEM((2, t, t), F32),
            pltpu.VMEM((2, 8, t), F32),
            pltpu.VMEM((2, 8, t), F32),
            pltpu.VMEM((2, 8, t), F32),
            pltpu.VMEM((2, acc_rows, t), F32),
            pltpu.VMEM((8, V_DIM), F32),
        ],
        compiler_params=_params("parallel", "parallel", "arbitrary"),
        name="diff_attention",
    )(*[v.reshape(1, HEAD_DIM) for v in lams], subw.reshape(1, V_DIM),
      y, y_meta_pad, y, y, y_meta_pad, y_meta_pad)


def _chan_dft_kernel(f_ref, cs_ref, o_ref):
    t = jnp.dot(f_ref[...], cs_ref[...], preferred_element_type=F32)
    o_ref[0] = t[:, :GROUP_DIM].astype(o_ref.dtype)
    o_ref[1] = t[:, GROUP_DIM:].astype(o_ref.dtype)


def _chan_dft(y, cs, *, tr):
    b, rows, in_width = y.shape
    f_blk0 = in_width // GROUP_DIM - N_GROUPS
    return pl.pallas_call(
        _chan_dft_kernel,
        grid=(b, rows // tr, N_GROUPS),
        in_specs=[
            pl.BlockSpec((None, tr, GROUP_DIM), lambda bi, r, g: (bi, r, f_blk0 + g)),
            pl.BlockSpec((GROUP_DIM, 2 * GROUP_DIM), lambda bi, r, g: (0, 0)),
        ],
        out_specs=pl.BlockSpec((None, 2, tr, GROUP_DIM), lambda bi, r, g: (bi, 0, r, g)),
        out_shape=jax.ShapeDtypeStruct((b, 2, rows, N_GROUPS * GROUP_DIM), BF16),
        compiler_params=_params("parallel", "parallel", "parallel"),
        name="chan_dft",
    )(y, cs)


def _pos_dft_kernel(mr_ref, mm_ref, t_ref, tm_ref, nw_ref, o_ref, *, scale):
    r = jnp.dot(mr_ref[...], t_ref[...], preferred_element_type=F32)
    r = r + jnp.dot(mm_ref[...], tm_ref[...], preferred_element_type=F32)
    r = r * scale
    inv = lax.rsqrt(jnp.mean(r * r, axis=-1, keepdims=True) + NORM_EPS)
    o_ref[...] = (r * inv * nw_ref[...]).astype(o_ref.dtype)


def _pos_dft(mr, mm, t, t_meta, nw, *, tp, scale):
    p_rows = mr.shape[0]
    b, k_real, width = t.shape
    k_meta = t_meta.shape[1]
    return pl.pallas_call(
        functools.partial(_pos_dft_kernel, scale=scale),
        grid=(b, N_GROUPS, p_rows // tp),
        in_specs=[
            pl.BlockSpec((tp, k_real), lambda bi, g, p: (p, 0)),
            pl.BlockSpec((tp, k_meta), lambda bi, g, p: (p, 0)),
            pl.BlockSpec((None, k_real, GROUP_DIM), lambda bi, g, p: (bi, 0, g)),
            pl.BlockSpec((None, k_meta, GROUP_DIM), lambda bi, g, p: (0, 0, g)),
            pl.BlockSpec((1, GROUP_DIM), lambda bi, g, p: (0, g)),
        ],
        out_specs=pl.BlockSpec((None, tp, GROUP_DIM), lambda bi, g, p: (bi, p, g)),
        out_shape=jax.ShapeDtypeStruct((b, p_rows, width), BF16),
        compiler_params=_params("parallel", "parallel", "arbitrary"),
        name="pos_dft",
    )(mr, mm, t, t_meta, nw.reshape(1, width))


TABLE_STEP = 64


def _dft_tables(seq):
    l_total = N_META + seq
    c = jnp.arange(GROUP_DIM, dtype=jnp.int32)
    ang_c = ((c[:, None] * c[None, :]) % GROUP_DIM).astype(F32) * (2.0 * math.pi / GROUP_DIM)
    cs = jnp.concatenate([jnp.cos(ang_c), jnp.sin(ang_c)], axis=1).astype(BF16)

    to_rad = 2.0 * math.pi / l_total
    half_pi = jnp.asarray([0.0, 0.5 * math.pi], F32)
    meta_p = jnp.arange(N_META, dtype=jnp.int32)
    real_p = N_META + jnp.arange(seq, dtype=jnp.int32)
    real_l, real_phase = jnp.tile(real_p, 2), jnp.repeat(half_pi, seq)
    meta_l, meta_phase = jnp.tile(meta_p, 2), jnp.repeat(half_pi, N_META)

    def angle(p, l, phase):
        return ((p[:, None] * l[None, :]) % l_total).astype(F32) * to_rad + phase[None, :]

    def direct(p, l, phase):
        return jnp.cos(angle(p, l, phase)).astype(BF16)

    hi = angle(N_META + TABLE_STEP * jnp.arange(seq // TABLE_STEP, dtype=jnp.int32), real_l, 0.0 * real_phase)
    lo = angle(jnp.arange(TABLE_STEP, dtype=jnp.int32), real_l, real_phase)
    real_rows_real_cols = (jnp.cos(hi)[:, None, :] * jnp.cos(lo)[None, :, :]
                           - jnp.sin(hi)[:, None, :] * jnp.sin(lo)[None, :, :]
                           ).reshape(seq, 2 * seq).astype(BF16)
    real_rows = (real_rows_real_cols, direct(real_p, meta_l, meta_phase))
    meta_rows = (direct(meta_p, real_l, real_phase), direct(meta_p, meta_l, meta_phase))
    return cs, real_rows, meta_rows


def _matmul_res_kernel(*refs, n_pairs):
    res_ref, o_ref = refs[2 * n_pairs], refs[2 * n_pairs + 1]
    acc = res_ref[...]
    for k in range(n_pairs):
        acc = acc + jnp.dot(refs[2 * k][...], refs[2 * k + 1][...], preferred_element_type=F32)
    o_ref[...] = acc


def _matmul_res(pairs, res, *, tm, tn):
    m, n = res.shape
    in_specs, args = [], []
    for a, w, blk in pairs:
        k = a.shape[1]
        in_specs += [pl.BlockSpec((tm, k), lambda i, j: (i, 0)),
                     pl.BlockSpec((k, tn), lambda i, j, blk=blk: (blk, j))]
        args += [a, w]
    in_specs.append(pl.BlockSpec((tm, tn), lambda i, j: (i, j)))
    return pl.pallas_call(
        functools.partial(_matmul_res_kernel, n_pairs=len(pairs)),
        grid=(m // tm, n // tn),
        in_specs=in_specs,
        out_specs=pl.BlockSpec((tm, tn), lambda i, j: (i, j)),
        out_shape=jax.ShapeDtypeStruct((m, n), F32),
        compiler_params=_params("parallel", "arbitrary"),
        name="matmul_res",
    )(*args, res)


HALO_ROWS = 8
NORM_ROWS = 256


def _up_glu_kernel(x_ref, nw_ref, wg_ref, wv_ref, halo_ref, cw_ref, cb_ref, o_ref, hn_ref, *, tm):
    @pl.when(pl.program_id(1) == 0)
    def _():
        def norm_rows(r, carry):
            rows = pl.ds(pl.multiple_of(r * NORM_ROWS, NORM_ROWS), NORM_ROWS)
            hn_ref[rows, :] = _norm_rows(x_ref, nw_ref, rows)
            return carry
        lax.fori_loop(0, tm // NORM_ROWS, norm_rows, 0)

    hn = hn_ref[...]
    g = jnp.dot(hn, wg_ref[...], preferred_element_type=F32)
    v = jnp.dot(hn, wv_ref[...], preferred_element_type=F32)
    row = lax.broadcasted_iota(jnp.int32, g.shape, 0)
    g_prev = jnp.where(row == 0, halo_ref[0, 0:1, :], pltpu.roll(g, 1, axis=0))
    g_next = jnp.where(row == tm - 1, halo_ref[0, 1:2, :], pltpu.roll(g, tm - 1, axis=0))
    conv = g_prev * cw_ref[0:1, :] + g * cw_ref[1:2, :] + g_next * cw_ref[2:3, :] + cb_ref[...]
    gelu = 0.5 * conv * (1.0 + lax.erf(conv * math.sqrt(0.5)))
    o_ref[...] = (gelu * v).astype(o_ref.dtype)


def _up_glu(x, nw, w_up, halo, conv_w, conv_b, *, tm, tc):
    m, d = x.shape
    d_ff = w_up.shape[1] // 2
    assert d_ff % LANES == 0 and d_ff >= tc
    n_c = pl.cdiv(d_ff, tc)
    el = pl.Element

    def col(c, base=0):
        return pl.multiple_of(base + jnp.minimum(c * tc, d_ff - tc), LANES)

    return pl.pallas_call(
        functools.partial(_up_glu_kernel, tm=tm),
        grid=(m // tm, n_c),
        in_specs=[
            pl.BlockSpec((tm, d), lambda i, c: (i, 0), pipeline_mode=pl.Buffered(1)),
            pl.BlockSpec((1, d), lambda i, c: (0, 0)),
            pl.BlockSpec((el(d), el(tc)), lambda i, c: (0, col(c))),
            pl.BlockSpec((el(d), el(tc)), lambda i, c: (0, col(c, d_ff))),
            pl.BlockSpec((el(1), el(HALO_ROWS), el(tc)), lambda i, c: (i, 0, col(c))),
            pl.BlockSpec((el(CONV_WIDTH), el(tc)), lambda i, c: (0, col(c))),
            pl.BlockSpec((el(1), el(tc)), lambda i, c: (0, col(c))),
        ],
        out_specs=pl.BlockSpec((el(tm), el(tc)), lambda i, c: (pl.multiple_of(i * tm, tm), col(c))),
        out_shape=jax.ShapeDtypeStruct((m, d_ff), BF16),
        scratch_shapes=[pltpu.VMEM((tm, d), BF16)],
        compiler_params=_params("parallel", "arbitrary"),
        name="up_glu",
    )(x, nw.reshape(1, d), w_up, w_up, halo, conv_w, conv_b.reshape(1, d_ff))


def _gate_halo(h_mid, h_mid_meta, nw, w_up, *, b, seq, tm, d_ff):
    d = h_mid.shape[-1]
    n_t = seq // tm
    h3 = h_mid.reshape(b, seq, d)
    pairs = [h3[:, k * tm - 1:k * tm + 1] for k in range(1, n_t)]
    meta_last = h_mid_meta.reshape(b, N_META, d)[:, N_META - 1:]
    rows = jnp.concatenate([meta_last] + pairs, axis=1).reshape(-1, d)
    n_rows = rows.shape[0]
    pad = -n_rows % 16
    rows = jnp.pad(rows, ((0, pad), (0, 0)))
    gates = _norm_matmul(rows, nw, w_up, tm=n_rows + pad, tn=256, n_cols=d_ff)[:n_rows]
    gates = gates.astype(F32).reshape(b, 2 * n_t - 1, d_ff)
    before = jnp.concatenate([gates[:, 0:1], gates[:, 1::2]], axis=1)
    after = jnp.concatenate([gates[:, 2::2], jnp.zeros((b, 1, d_ff), F32)], axis=1)
    halo = jnp.stack([before, after], axis=2)
    halo = jnp.pad(halo, ((0, 0), (0, 0), (0, HALO_ROWS - 2), (0, 0)))
    return halo.reshape(b * n_t, HALO_ROWS, d_ff)


def _rmsnorm_kernel(x_ref, w_ref, o_ref):
    x = x_ref[...]
    inv = lax.rsqrt(jnp.mean(x * x, axis=-1, keepdims=True) + NORM_EPS)
    o_ref[...] = x * inv * w_ref[...]


def _rmsnorm(x, w, *, tm):
    m, d = x.shape
    return pl.pallas_call(
        _rmsnorm_kernel,
        grid=(m // tm,),
        in_specs=[pl.BlockSpec((tm, d), lambda i: (i, 0)), pl.BlockSpec((1, d), lambda i: (0, 0))],
        out_specs=pl.BlockSpec((tm, d), lambda i: (i, 0)),
        out_shape=jax.ShapeDtypeStruct((m, d), F32),
        compiler_params=_params("parallel"),
        name="final_rmsnorm",
    )(x, w.reshape(1, d))


def kernel(x, meta_tokens, norm1_w, w_in, lambda_q1, lambda_k1, lambda_q2, lambda_k2,
           diff_subln_w, fourier_norm_w, w_out, norm2_w, w_up, conv_w, conv_b, w_down,
           final_norm_w):
    b, seq, d = x.shape
    m = b * seq
    in_width = w_in.shape[-1]
    d_ff = w_down.shape[1]
    assert w_in.shape[0] == 1 and meta_tokens.shape[0] == N_META

    w_in_b = w_in[0].astype(BF16)
    w_out_b = w_out[0].astype(BF16)
    w_up_b = w_up[0].astype(BF16)
    w_down_b = w_down[0].astype(BF16)
    lams = (lambda_q1[0], lambda_k1[0], lambda_q2[0], lambda_k2[0])

    y = _norm_matmul(x.reshape(m, d), norm1_w[0], w_in_b, tm=512, tn=1024).reshape(b, seq, in_width)
    y_meta = _norm_matmul(meta_tokens, norm1_w[0], w_in_b, tm=N_META, tn=1024)

    y_meta_pad = jnp.pad(y_meta, ((0, META_PAD - N_META), (0, 0)))
    attn, attn_meta = _attention(lams, diff_subln_w[0], y, y_meta_pad, t=min(512, seq))
    attn_meta = attn_meta[:, :N_META]

    cs, real_rows, meta_rows = _dft_tables(seq)
    four_w = N_GROUPS * GROUP_DIM
    t = _chan_dft(y, cs, tr=min(1024, seq)).reshape(b, 2 * seq, four_w)
    t_meta = _chan_dft(y_meta[None], cs, tr=N_META).reshape(1, 2 * N_META, four_w)
    scale = 1.0 / math.sqrt((N_META + seq) * GROUP_DIM)
    four = _pos_dft(*real_rows, t, t_meta, fourier_norm_w[0], tp=512, scale=scale)
    four_meta = _pos_dft(*meta_rows, t, t_meta, fourier_norm_w[0], tp=N_META, scale=scale)

    attn_w = attn.shape[-1]
    h_mid = _matmul_res([(attn.reshape(m, attn_w), w_out_b, 0), (four.reshape(m, four_w), w_out_b, 1)],
                        x.reshape(m, d), tm=512, tn=min(1024, d))
    meta_res = jnp.broadcast_to(meta_tokens[None], (b, N_META, d)).reshape(b * N_META, d)
    h_mid_meta = _matmul_res([(attn_meta.reshape(b * N_META, attn_w), w_out_b, 0),
                              (four_meta.reshape(b * N_META, four_w), w_out_b, 1)],
                             meta_res, tm=b * N_META, tn=min(1024, d))

    tm_up = min(1024, seq)
    halo = _gate_halo(h_mid, h_mid_meta, norm2_w[0], w_up_b, b=b, seq=seq, tm=tm_up, d_ff=d_ff)
    z = _up_glu(h_mid, norm2_w[0], w_up_b, halo, conv_w[0], conv_b[0], tm=tm_up, tc=512)
    h_out = _matmul_res([(z, w_down_b, 0)], h_mid, tm=512, tn=min(512, d))

    return _rmsnorm(h_out, final_norm_w, tm=512).reshape(b, seq, d)
```

```python
import functools
import math

import jax
import jax.numpy as jnp
from jax import lax
from jax.experimental import pallas as pl
from jax.experimental.pallas import tpu as pltpu

N_META = 16
N_HEADS = 16
HEAD_DIM = 64
V_DIM = 128
LANES = 128
N_GROUPS = 4
GROUP_DIM = 512
CONV_WIDTH = 3
NORM_EPS = 1e-6
SUBLN_EPS = 1e-5
LAMBDA_INIT = 0.8 - 0.6 * math.exp(-0.3 * 0)

VMEM_LIMIT_BYTES = 56 * 1024 * 1024
BF16 = jnp.bfloat16
F32 = jnp.float32


def _params(*sem):
    return pltpu.CompilerParams(dimension_semantics=sem, vmem_limit_bytes=VMEM_LIMIT_BYTES)


def _norm_rows(x_ref, nw_ref, rows):
    x = x_ref[rows, :]
    inv = lax.rsqrt(jnp.mean(x * x, axis=-1, keepdims=True) + NORM_EPS)
    return (x * inv * nw_ref[...]).astype(BF16)


def _norm_matmul_kernel(x_ref, nw_ref, w_ref, o_ref, hn_ref, *, row_chunks):
    @pl.when(pl.program_id(1) == 0)
    def _():
        for rows in row_chunks:
            hn = _norm_rows(x_ref, nw_ref, rows)
            hn_ref[rows, :] = hn
            o_ref[rows, :] = jnp.dot(hn, w_ref[...], preferred_element_type=F32).astype(o_ref.dtype)

    @pl.when(pl.program_id(1) != 0)
    def _():
        o_ref[...] = jnp.dot(hn_ref[...], w_ref[...], preferred_element_type=F32).astype(o_ref.dtype)


def _norm_matmul(x, nw, w, *, tm, tn, n_cols=None):
    m, d = x.shape
    n = w.shape[1] if n_cols is None else n_cols
    step = min(tm, NORM_ROWS)
    row_chunks = tuple(slice(r, r + step) for r in range(0, tm, step))
    return pl.pallas_call(
        functools.partial(_norm_matmul_kernel, row_chunks=row_chunks),
        grid=(m // tm, n // tn),
        in_specs=[
            pl.BlockSpec((tm, d), lambda i, j: (i, 0)),
            pl.BlockSpec((1, d), lambda i, j: (0, 0)),
            pl.BlockSpec((d, tn), lambda i, j: (0, j)),
        ],
        out_specs=pl.BlockSpec((tm, tn), lambda i, j: (i, j)),
        out_shape=jax.ShapeDtypeStruct((m, n), BF16),
        scratch_shapes=[pltpu.VMEM((tm, d), BF16)],
        compiler_params=_params("parallel", "arbitrary"),
        name="norm_matmul",
    )(x, nw.reshape(1, d), w)


META_PAD = 128
SUM_ROWS = 16
MASKED = -1e30
LOG2E = 1.0 / math.log(2.0)
BEFORE, DIAG, AFTER = 0, 1, 2
SKIP_BITS = 160.0
NORM_SLACK = 1.01


def _attn_kernel(lq1_ref, lk1_ref, lq2_ref, lk2_ref, subw_ref, q_ref, qm_ref, k_ref, v_ref,
                 km_ref, vm_ref, o_ref, om_ref, vt_ref, vtm_ref, bias_ref, ta_ref, tb_ref,
                 ma_ref, mb_ref, m_ref, acc_ref, knorm_ref, *, t):
    h = pl.program_id(1)
    qi = pl.program_id(2)
    seq = k_ref.shape[0]
    n_chunks = seq // t
    assert n_chunks >= 2, "every query tile needs a neighbouring key chunk"
    acc_rows = V_DIM + SUM_ROWS

    slope = LOG2E * jnp.exp2(jnp.full((1, 1), -8.0 / N_HEADS, F32) * jnp.asarray(h + 1, F32))
    lane = lax.broadcasted_iota(jnp.int32, (1, V_DIM), 1)
    map_cols = (lax.broadcasted_iota(jnp.int32, (V_DIM, V_DIM), 0) // HEAD_DIM
                == lax.broadcasted_iota(jnp.int32, (V_DIM, V_DIM), 1)).astype(BF16)
    map_rows = (lax.broadcasted_iota(jnp.int32, (8, V_DIM), 1) // HEAD_DIM
                == lax.broadcasted_iota(jnp.int32, (8, V_DIM), 0)).astype(BF16)

    @pl.when(qi == 0)
    def _():
        for c in range(n_chunks):
            vt_ref[0:V_DIM, c * t:(c + 1) * t] = v_ref[c * t:(c + 1) * t, :].astype(F32).T.astype(BF16)
        vt_ref[V_DIM:acc_rows, :] = jnp.ones((SUM_ROWS, seq), BF16)
        vtm_ref[0:V_DIM, :] = vm_ref[...].astype(F32).T.astype(BF16)
        vtm_ref[V_DIM:acc_rows, :] = jnp.ones((SUM_ROWS, META_PAD), BF16)
        row = lax.broadcasted_iota(jnp.int32, (t, t), 0)
        col = lax.broadcasted_iota(jnp.int32, (t, t), 1)
        bias_ref[BEFORE] = slope * row.astype(F32)
        bias_ref[DIAG] = -slope * jnp.abs(col - row).astype(F32)
        bias_ref[AFTER] = -slope * row.astype(F32)
        k_sq = jnp.square(k_ref[...].astype(F32)).astype(BF16)
        sq_norms = jnp.dot(k_sq, map_cols, preferred_element_type=F32)
        knorm_ref[0:1, :] = jnp.sqrt(jnp.max(sq_norms, axis=0, keepdims=True))

    lam = (jnp.exp(jnp.sum(lq1_ref[...] * lk1_ref[...], axis=-1, keepdims=True))
           - jnp.exp(jnp.sum(lq2_ref[...] * lk2_ref[...], axis=-1, keepdims=True))
           + LAMBDA_INIT)
    meta_row = lax.broadcasted_iota(jnp.int32, (META_PAD, 1), 0)
    meta_valid = meta_row < N_META

    def attend(q, q_pos, tile_idx, out_ref):
        nq = q.shape[0]
        q32 = q.astype(F32) * (LOG2E / math.sqrt(HEAD_DIM))
        qs = (jnp.where(lane < HEAD_DIM, q32, 0.0).astype(BF16),
              jnp.where(lane >= HEAD_DIM, q32, 0.0).astype(BF16))

        def scores(k_c):
            return tuple(lax.dot_general(k_c, qc, (((1,), (1,)), ((), ())), preferred_element_type=F32)
                         for qc in qs)

        def update(c, tt, tt_max, row_bias, vt_c):
            m = m_ref[c, 0:1, 0:nq]
            m_new = jnp.maximum(m, tt_max + row_bias)
            p = jnp.exp2(tt - (m_new - row_bias))
            acc_ref[c, :, 0:nq] = (jnp.exp2(m - m_new) * acc_ref[c, :, 0:nq]
                                   + jnp.dot(vt_c, p.astype(BF16), preferred_element_type=F32))
            m_ref[c, 0:1, 0:nq] = m_new

        def produce(c_idx, kind, buf_ref, max_ref):
            k_c = k_ref[pl.ds(pl.multiple_of(c_idx * t, t), t), :]
            key_bias = bias_ref[kind, :, 0:nq]
            for c, s in enumerate(scores(k_c)):
                tt = s + key_bias
                buf_ref[c, :, 0:nq] = tt
                max_ref[c, 0:1, 0:nq] = jnp.max(tt, axis=0, keepdims=True)

        def consume(c_idx, kind, buf_ref, max_ref):
            sign = 1.0 - kind
            k_pos0 = jnp.asarray(N_META + c_idx * t, F32)
            row_bias = (-sign) * slope * (q_pos - k_pos0)
            vt_c = vt_ref[:, pl.ds(pl.multiple_of(c_idx * t, t), t)]
            for c in range(2):
                update(c, buf_ref[c, :, 0:nq], max_ref[c, 0:1, 0:nq], row_bias, vt_c)

        m_ref[...] = jnp.full(m_ref.shape, -jnp.inf, F32)
        acc_ref[...] = jnp.zeros(acc_ref.shape, F32)
        buf_a, buf_b = (ta_ref, ma_ref), (tb_ref, mb_ref)
        if tile_idx is None:
            first = (jnp.int32(0), AFTER)
        else:
            produce(tile_idx, DIAG, *buf_a)
            has_before = tile_idx > 0
            first = (jnp.where(has_before, tile_idx - 1, tile_idx + 1),
                     jnp.where(has_before, BEFORE, AFTER))

        if tile_idx is None:
            key_bias, row_bias = -slope * jnp.abs(q_pos - meta_row.astype(F32)), jnp.zeros((1, nq), F32)
        else:
            key_bias, row_bias = bias_ref[BEFORE, 0:META_PAD, 0:nq], -slope * q_pos
        for c, s in enumerate(scores(km_ref[...])):
            tt = jnp.where(meta_valid, s + key_bias, MASKED)
            update(c, tt, jnp.max(tt, axis=0, keepdims=True), row_bias, vtm_ref[...])

        produce(*first, *buf_b)
        if tile_idx is not None:
            consume(tile_idx, DIAG, *buf_a)

        q_sq = jnp.square(q32.astype(BF16).astype(F32)).astype(BF16)
        q_norm = jnp.sqrt(lax.dot_general(map_rows, q_sq, (((1,), (1,)), ((), ())),
                                          preferred_element_type=F32))
        bound = jnp.full((1, 1), -jnp.inf, F32)
        for c in range(2):
            gap = knorm_ref[0:1, c:c + 1] * q_norm[c:c + 1] * NORM_SLACK - m_ref[c, 0:1, 0:nq]
            bound = jnp.maximum(bound, jnp.max(gap, axis=-1, keepdims=True))
        reach = ((bound + SKIP_BITS) / slope - 1.0) / t + 1.0
        n_side = jnp.max(jnp.clip(jnp.floor(reach), 1.0, float(n_chunks))).astype(jnp.int32)
        if tile_idx is None:
            n_before, n_todo = 0, jnp.minimum(n_chunks, n_side)
        else:
            n_before = jnp.minimum(tile_idx, n_side)
            n_todo = n_before + jnp.minimum(n_chunks - 1 - tile_idx, n_side)

        def todo(k):
            if tile_idx is None:
                return k, AFTER
            before = k < n_before
            c_idx = jnp.where(before, tile_idx - 1 - k, tile_idx + 1 + k - n_before)
            return c_idx, jnp.where(before, BEFORE, AFTER)

        def pair(j, carry):
            k = 2 * j
            produce(*todo(k + 1), *buf_a)
            consume(*todo(k), *buf_b)
            produce(*todo(jnp.minimum(k + 2, n_todo - 1)), *buf_b)
            consume(*todo(k + 1), *buf_a)
            return carry

        lax.fori_loop(0, n_todo // 2, pair, 0)

        @pl.when(n_todo % 2 == 1)
        def _():
            consume(*todo(n_todo - 1), *buf_b)

        acc1, acc2 = acc_ref[0, :, 0:nq], acc_ref[1, :, 0:nq]
        a = acc1[0:V_DIM] / acc1[V_DIM:V_DIM + 1] - lam * (acc2[0:V_DIM] / acc2[V_DIM:V_DIM + 1])
        inv = lax.rsqrt(jnp.mean(a * a, axis=0, keepdims=True) + SUBLN_EPS)
        out_ref[...] = ((a * inv).T * subw_ref[...] * (1.0 - LAMBDA_INIT)).astype(out_ref.dtype)

    q_pos = (N_META + qi * t + lax.broadcasted_iota(jnp.int32, (1, t), 1)).astype(F32)
    attend(q_ref[...], q_pos, qi, o_ref)

    @pl.when(qi == 0)
    def _():
        pos = jnp.minimum(lax.broadcasted_iota(jnp.int32, (1, META_PAD), 1), N_META - 1)
        attend(qm_ref[...], pos.astype(F32), None, om_ref)


def _attention(lams, subw, y, y_meta_pad, *, t):
    b, seq, _ = y.shape
    h_blk = N_HEADS
    vec = lambda n: pl.BlockSpec((1, n), lambda bi, h, i: (0, 0))
    acc_rows = V_DIM + SUM_ROWS
    return pl.pallas_call(
        functools.partial(_attn_kernel, t=t),
        grid=(b, N_HEADS, seq // t),
        in_specs=[
            vec(HEAD_DIM), vec(HEAD_DIM), vec(HEAD_DIM), vec(HEAD_DIM), vec(V_DIM),
            pl.BlockSpec((None, t, V_DIM), lambda bi, h, i: (bi, i, h)),
            pl.BlockSpec((META_PAD, V_DIM), lambda bi, h, i: (0, h)),
            pl.BlockSpec((None, seq, V_DIM), lambda bi, h, i: (bi, 0, h_blk + h)),
            pl.BlockSpec((None, seq, V_DIM), lambda bi, h, i: (bi, 0, 2 * h_blk + h)),
            pl.BlockSpec((META_PAD, V_DIM), lambda bi, h, i: (0, h_blk + h)),
            pl.BlockSpec((META_PAD, V_DIM), lambda bi, h, i: (0, 2 * h_blk + h)),
        ],
        out_specs=[
            pl.BlockSpec((None, t, V_DIM), lambda bi, h, i: (bi, i, h)),
            pl.BlockSpec((None, META_PAD, V_DIM), lambda bi, h, i: (bi, 0, h)),
        ],
        out_shape=[
            jax.ShapeDtypeStruct((b, seq, N_HEADS * V_DIM), BF16),
            jax.ShapeDtypeStruct((b, META_PAD, N_HEADS * V_DIM), BF16),
        ],
        scratch_shapes=[
            pltpu.VMEM((acc_rows, seq), BF16),
            pltpu.VMEM((acc_rows, META_PAD), BF16),
            pltpu.VMEM((3, t, t), F32),
            pltpu.VMEM((2, t, t), F32),
            pltpu.VMEM((2, t, t), F32),
            pltpu.VMEM((2, 8, t), F32),
            pltpu.VMEM((2, 8, t), F32),
            pltpu.VMEM((2, 8, t), F32),
            pltpu.VMEM((2, acc_rows, t), F32),
            pltpu.VMEM((8, V_DIM), F32),
        ],
        compiler_params=_params("parallel", "parallel", "arbitrary"),
        name="diff_attention",
    )(*[v.reshape(1, HEAD_DIM) for v in lams], subw.reshape(1, V_DIM),
      y, y_meta_pad, y, y, y_meta_pad, y_meta_pad)


def _chan_dft_kernel(f_ref, cs_ref, o_ref):
    t = jnp.dot(f_ref[...], cs_ref[...], preferred_element_type=F32)
    o_ref[0] = t[:, :GROUP_DIM].astype(o_ref.dtype)
    o_ref[1] = t[:, GROUP_DIM:].astype(o_ref.dtype)


def _chan_dft(y, cs, *, tr):
    b, rows, in_width = y.shape
    f_blk0 = in_width // GROUP_DIM - N_GROUPS
    return pl.pallas_call(
        _chan_dft_kernel,
        grid=(b, rows // tr, N_GROUPS),
        in_specs=[
            pl.BlockSpec((None, tr, GROUP_DIM), lambda bi, r, g: (bi, r, f_blk0 + g)),
            pl.BlockSpec((GROUP_DIM, 2 * GROUP_DIM), lambda bi, r, g: (0, 0)),
        ],
        out_specs=pl.BlockSpec((None, 2, tr, GROUP_DIM), lambda bi, r, g: (bi, 0, r, g)),
        out_shape=jax.ShapeDtypeStruct((b, 2, rows, N_GROUPS * GROUP_DIM), BF16),
        compiler_params=_params("parallel", "parallel", "parallel"),
        name="chan_dft",
    )(y, cs)


def _pos_dft_kernel(mr_ref, mm_ref, t_ref, tm_ref, nw_ref, o_ref, *, scale):
    r = jnp.dot(mr_ref[...], t_ref[...], preferred_element_type=F32)
    r = r + jnp.dot(mm_ref[...], tm_ref[...], preferred_element_type=F32)
    r = r * scale
    inv = lax.rsqrt(jnp.mean(r * r, axis=-1, keepdims=True) + NORM_EPS)
    o_ref[...] = (r * inv * nw_ref[...]).astype(o_ref.dtype)


def _pos_dft(mr, mm, t, t_meta, nw, *, tp, scale):
    p_rows = mr.shape[0]
    b, k_real, width = t.shape
    k_meta = t_meta.shape[1]
    return pl.pallas_call(
        functools.partial(_pos_dft_kernel, scale=scale),
        grid=(b, N_GROUPS, p_rows // tp),
        in_specs=[
            pl.BlockSpec((tp, k_real), lambda bi, g, p: (p, 0)),
            pl.BlockSpec((tp, k_meta), lambda bi, g, p: (p, 0)),
            pl.BlockSpec((None, k_real, GROUP_DIM), lambda bi, g, p: (bi, 0, g)),
            pl.BlockSpec((None, k_meta, GROUP_DIM), lambda bi, g, p: (0, 0, g)),
            pl.BlockSpec((1, GROUP_DIM), lambda bi, g, p: (0, g)),
        ],
        out_specs=pl.BlockSpec((None, tp, GROUP_DIM), lambda bi, g, p: (bi, p, g)),
        out_shape=jax.ShapeDtypeStruct((b, p_rows, width), BF16),
        compiler_params=_params("parallel", "parallel", "arbitrary"),
        name="pos_dft",
    )(mr, mm, t, t_meta, nw.reshape(1, width))


TABLE_STEP = 64


def _dft_tables(seq):
    l_total = N_META + seq
    c = jnp.arange(GROUP_DIM, dtype=jnp.int32)
    ang_c = ((c[:, None] * c[None, :]) % GROUP_DIM).astype(F32) * (2.0 * math.pi / GROUP_DIM)
    cs = jnp.concatenate([jnp.cos(ang_c), jnp.sin(ang_c)], axis=1).astype(BF16)

    to_rad = 2.0 * math.pi / l_total
    half_pi = jnp.asarray([0.0, 0.5 * math.pi], F32)
    meta_p = jnp.arange(N_META, dtype=jnp.int32)
    real_p = N_META + jnp.arange(seq, dtype=jnp.int32)
    real_l, real_phase = jnp.tile(real_p, 2), jnp.repeat(half_pi, seq)
    meta_l, meta_phase = jnp.tile(meta_p, 2), jnp.repeat(half_pi, N_META)

    def angle(p, l, phase):
        return ((p[:, None] * l[None, :]) % l_total).astype(F32) * to_rad + phase[None, :]

    def direct(p, l, phase):
        return jnp.cos(angle(p, l, phase)).astype(BF16)

    hi = angle(N_META + TABLE_STEP * jnp.arange(seq // TABLE_STEP, dtype=jnp.int32), real_l, 0.0 * real_phase)
    lo = angle(jnp.arange(TABLE_STEP, dtype=jnp.int32), real_l, real_phase)
    real_rows_real_cols = (jnp.cos(hi)[:, None, :] * jnp.cos(lo)[None, :, :]
                           - jnp.sin(hi)[:, None, :] * jnp.sin(lo)[None, :, :]
                           ).reshape(seq, 2 * seq).astype(BF16)
    real_rows = (real_rows_real_cols, direct(real_p, meta_l, meta_phase))
    meta_rows = (direct(meta_p, real_l, real_phase), direct(meta_p, meta_l, meta_phase))
    return cs, real_rows, meta_rows


def _matmul_res_kernel(*refs, n_pairs):
    res_ref, o_ref = refs[2 * n_pairs], refs[2 * n_pairs + 1]
    acc = res_ref[...]
    for k in range(n_pairs):
        acc = acc + jnp.dot(refs[2 * k][...], refs[2 * k + 1][...], preferred_element_type=F32)
    o_ref[...] = acc


def _matmul_res(pairs, res, *, tm, tn):
    m, n = res.shape
    in_specs, args = [], []
    for a, w, blk in pairs:
        k = a.shape[1]
        in_specs += [pl.BlockSpec((tm, k), lambda i, j: (i, 0)),
                     pl.BlockSpec((k, tn), lambda i, j, blk=blk: (blk, j))]
        args += [a, w]
    in_specs.append(pl.BlockSpec((tm, tn), lambda i, j: (i, j)))
    return pl.pallas_call(
        functools.partial(_matmul_res_kernel, n_pairs=len(pairs)),
        grid=(m // tm, n // tn),
        in_specs=in_specs,
        out_specs=pl.BlockSpec((tm, tn), lambda i, j: (i, j)),
        out_shape=jax.ShapeDtypeStruct((m, n), F32),
        compiler_params=_params("parallel", "arbitrary"),
        name="matmul_res",
    )(*args, res)


HALO_ROWS = 8
NORM_ROWS = 256


def _up_glu_kernel(x_ref, nw_ref, wg_ref, wv_ref, halo_ref, cw_ref, cb_ref, o_ref, hn_ref, *, tm):
    @pl.when(pl.program_id(1) == 0)
    def _():
        def norm_rows(r, carry):
            rows = pl.ds(pl.multiple_of(r * NORM_ROWS, NORM_ROWS), NORM_ROWS)
            hn_ref[rows, :] = _norm_rows(x_ref, nw_ref, rows)
            return carry
        lax.fori_loop(0, tm // NORM_ROWS, norm_rows, 0)

    hn = hn_ref[...]
    g = jnp.dot(hn, wg_ref[...], preferred_element_type=F32)
    v = jnp.dot(hn, wv_ref[...], preferred_element_type=F32)
    row = lax.broadcasted_iota(jnp.int32, g.shape, 0)
    g_prev = jnp.where(row == 0, halo_ref[0, 0:1, :], pltpu.roll(g, 1, axis=0))
    g_next = jnp.where(row == tm - 1, halo_ref[0, 1:2, :], pltpu.roll(g, tm - 1, axis=0))
    conv = g_prev * cw_ref[0:1, :] + g * cw_ref[1:2, :] + g_next * cw_ref[2:3, :] + cb_ref[...]
    gelu = 0.5 * conv * (1.0 + lax.erf(conv * math.sqrt(0.5)))
    o_ref[...] = (gelu * v).astype(o_ref.dtype)


def _up_glu(x, nw, w_up, halo, conv_w, conv_b, *, tm, tc):
    m, d = x.shape
    d_ff = w_up.shape[1] // 2
    assert d_ff % LANES == 0 and d_ff >= tc
    n_c = pl.cdiv(d_ff, tc)
    el = pl.Element

    def col(c, base=0):
        return pl.multiple_of(base + jnp.minimum(c * tc, d_ff - tc), LANES)

    return pl.pallas_call(
        functools.partial(_up_glu_kernel, tm=tm),
        grid=(m // tm, n_c),
        in_specs=[
            pl.BlockSpec((tm, d), lambda i, c: (i, 0), pipeline_mode=pl.Buffered(1)),
            pl.BlockSpec((1, d), lambda i, c: (0, 0)),
            pl.BlockSpec((el(d), el(tc)), lambda i, c: (0, col(c))),
            pl.BlockSpec((el(d), el(tc)), lambda i, c: (0, col(c, d_ff))),
            pl.BlockSpec((el(1), el(HALO_ROWS), el(tc)), lambda i, c: (i, 0, col(c))),
            pl.BlockSpec((el(CONV_WIDTH), el(tc)), lambda i, c: (0, col(c))),
            pl.BlockSpec((el(1), el(tc)), lambda i, c: (0, col(c))),
        ],
        out_specs=pl.BlockSpec((el(tm), el(tc)), lambda i, c: (pl.multiple_of(i * tm, tm), col(c))),
        out_shape=jax.ShapeDtypeStruct((m, d_ff), BF16),
        scratch_shapes=[pltpu.VMEM((tm, d), BF16)],
        compiler_params=_params("parallel", "arbitrary"),
        name="up_glu",
    )(x, nw.reshape(1, d), w_up, w_up, halo, conv_w, conv_b.reshape(1, d_ff))


def _gate_halo(h_mid, h_mid_meta, nw, w_up, *, b, seq, tm, d_ff):
    d = h_mid.shape[-1]
    n_t = seq // tm
    h3 = h_mid.reshape(b, seq, d)
    pairs = [h3[:, k * tm - 1:k * tm + 1] for k in range(1, n_t)]
    meta_last = h_mid_meta.reshape(b, N_META, d)[:, N_META - 1:]
    rows = jnp.concatenate([meta_last] + pairs, axis=1).reshape(-1, d)
    n_rows = rows.shape[0]
    pad = -n_rows % 16
    rows = jnp.pad(rows, ((0, pad), (0, 0)))
    gates = _norm_matmul(rows, nw, w_up, tm=n_rows + pad, tn=256, n_cols=d_ff)[:n_rows]
    gates = gates.astype(F32).reshape(b, 2 * n_t - 1, d_ff)
    before = jnp.concatenate([gates[:, 0:1], gates[:, 1::2]], axis=1)
    after = jnp.concatenate([gates[:, 2::2], jnp.zeros((b, 1, d_ff), F32)], axis=1)
    halo = jnp.stack([before, after], axis=2)
    halo = jnp.pad(halo, ((0, 0), (0, 0), (0, HALO_ROWS - 2), (0, 0)))
    return halo.reshape(b * n_t, HALO_ROWS, d_ff)


def _rmsnorm_kernel(x_ref, w_ref, o_ref):
    x = x_ref[...]
    inv = lax.rsqrt(jnp.mean(x * x, axis=-1, keepdims=True) + NORM_EPS)
    o_ref[...] = x * inv * w_ref[...]


def _rmsnorm(x, w, *, tm):
    m, d = x.shape
    return pl.pallas_call(
        _rmsnorm_kernel,
        grid=(m // tm,),
        in_specs=[pl.BlockSpec((tm, d), lambda i: (i, 0)), pl.BlockSpec((1, d), lambda i: (0, 0))],
        out_specs=pl.BlockSpec((tm, d), lambda i: (i, 0)),
        out_shape=jax.ShapeDtypeStruct((m, d), F32),
        compiler_params=_params("parallel"),
        name="final_rmsnorm",
    )(x, w.reshape(1, d))


def kernel(x, meta_tokens, norm1_w, w_in, lambda_q1, lambda_k1, lambda_q2, lambda_k2,
           diff_subln_w, fourier_norm_w, w_out, norm2_w, w_up, conv_w, conv_b, w_down,
           final_norm_w):
    b, seq, d = x.shape
    m = b * seq
    in_width = w_in.shape[-1]
    d_ff = w_down.shape[1]
    assert w_in.shape[0] == 1 and meta_tokens.shape[0] == N_META

    w_in_b = w_in[0].astype(BF16)
    w_out_b = w_out[0].astype(BF16)
    w_up_b = w_up[0].astype(BF16)
    w_down_b = w_down[0].astype(BF16)
    lams = (lambda_q1[0], lambda_k1[0], lambda_q2[0], lambda_k2[0])

    y = _norm_matmul(x.reshape(m, d), norm1_w[0], w_in_b, tm=512, tn=1024).reshape(b, seq, in_width)
    y_meta = _norm_matmul(meta_tokens, norm1_w[0], w_in_b, tm=N_META, tn=1024)

    y_meta_pad = jnp.pad(y_meta, ((0, META_PAD - N_META), (0, 0)))
    attn, attn_meta = _attention(lams, diff_subln_w[0], y, y_meta_pad, t=min(512, seq))
    attn_meta = attn_meta[:, :N_META]

    cs, real_rows, meta_rows = _dft_tables(seq)
    four_w = N_GROUPS * GROUP_DIM
    t = _chan_dft(y, cs, tr=min(1024, seq)).reshape(b, 2 * seq, four_w)
    t_meta = _chan_dft(y_meta[None], cs, tr=N_META).reshape(1, 2 * N_META, four_w)
    scale = 1.0 / math.sqrt((N_META + seq) * GROUP_DIM)
    four = _pos_dft(*real_rows, t, t_meta, fourier_norm_w[0], tp=512, scale=scale)
    four_meta = _pos_dft(*meta_rows, t, t_meta, fourier_norm_w[0], tp=N_META, scale=scale)

    attn_w = attn.shape[-1]
    h_mid = _matmul_res([(attn.reshape(m, attn_w), w_out_b, 0), (four.reshape(m, four_w), w_out_b, 1)],
                        x.reshape(m, d), tm=512, tn=min(1024, d))
    meta_res = jnp.broadcast_to(meta_tokens[None], (b, N_META, d)).reshape(b * N_META, d)
    h_mid_meta = _matmul_res([(attn_meta.reshape(b * N_META, attn_w), w_out_b, 0),
                              (four_meta.reshape(b * N_META, four_w), w_out_b, 1)],
                             meta_res, tm=b * N_META, tn=min(1024, d))

    tm_up = min(1024, seq)
    halo = _gate_halo(h_mid, h_mid_meta, norm2_w[0], w_up_b, b=b, seq=seq, tm=tm_up, d_ff=d_ff)
    z = _up_glu(h_mid, norm2_w[0], w_up_b, halo, conv_w[0], conv_b[0], tm=tm_up, tc=512)
    h_out = _matmul_res([(z, w_down_b, 0)], h_mid, tm=512, tn=min(512, d))

    return _rmsnorm(h_out, final_norm_w, tm=512).reshape(b, seq, d)
```

```python
import functools
import math

import jax
import jax.numpy as jnp
from jax import lax
from jax.experimental import pallas as pl
from jax.experimental.pallas import tpu as pltpu

N_META = 16
N_HEADS = 16
HEAD_DIM = 64
V_DIM = 128
LANES = 128
N_GROUPS = 4
GROUP_DIM = 512
CONV_WIDTH = 3
NORM_EPS = 1e-6
SUBLN_EPS = 1e-5
LAMBDA_INIT = 0.8 - 0.6 * math.exp(-0.3 * 0)

VMEM_LIMIT_BYTES = 56 * 1024 * 1024
BF16 = jnp.bfloat16
F32 = jnp.float32


def _params(*sem):
    return pltpu.CompilerParams(dimension_semantics=sem, vmem_limit_bytes=VMEM_LIMIT_BYTES)


def _norm_rows(x_ref, nw_ref, rows):
    x = x_ref[rows, :]
    inv = lax.rsqrt(jnp.mean(x * x, axis=-1, keepdims=True) + NORM_EPS)
    return (x * inv * nw_ref[...]).astype(BF16)


def _norm_matmul_kernel(x_ref, nw_ref, w_ref, o_ref, hn_ref, *, row_chunks):
    @pl.when(pl.program_id(1) == 0)
    def _():
        for rows in row_chunks:
            hn = _norm_rows(x_ref, nw_ref, rows)
            hn_ref[rows, :] = hn
            o_ref[rows, :] = jnp.dot(hn, w_ref[...], preferred_element_type=F32).astype(o_ref.dtype)

    @pl.when(pl.program_id(1) != 0)
    def _():
        o_ref[...] = jnp.dot(hn_ref[...], w_ref[...], preferred_element_type=F32).astype(o_ref.dtype)


def _norm_matmul(x, nw, w, *, tm, tn, n_cols=None):
    m, d = x.shape
    n = w.shape[1] if n_cols is None else n_cols
    step = min(tm, NORM_ROWS)
    row_chunks = tuple(slice(r, r + step) for r in range(0, tm, step))
    return pl.pallas_call(
        functools.partial(_norm_matmul_kernel, row_chunks=row_chunks),
        grid=(m // tm, n // tn),
        in_specs=[
            pl.BlockSpec((tm, d), lambda i, j: (i, 0)),
            pl.BlockSpec((1, d), lambda i, j: (0, 0)),
            pl.BlockSpec((d, tn), lambda i, j: (0, j)),
        ],
        out_specs=pl.BlockSpec((tm, tn), lambda i, j: (i, j)),
        out_shape=jax.ShapeDtypeStruct((m, n), BF16),
        scratch_shapes=[pltpu.VMEM((tm, d), BF16)],
        compiler_params=_params("parallel", "arbitrary"),
        name="norm_matmul",
    )(x, nw.reshape(1, d), w)


META_PAD = 128
SUM_ROWS = 16
MASKED = -1e30
LOG2E = 1.0 / math.log(2.0)
BEFORE, DIAG, AFTER = 0, 1, 2
SKIP_BITS = 160.0
NORM_SLACK = 1.01


def _attn_kernel(*refs, t, n_cast):
    (lq1_ref, lk1_ref, lq2_ref, lk2_ref, subw_ref, q_ref, qm_ref, k_ref, v_ref, km_ref,
     vm_ref) = refs[:N_ATTN_INPUTS]
    cast_in = refs[N_ATTN_INPUTS:N_ATTN_INPUTS + n_cast]
    o_ref, om_ref = refs[N_ATTN_INPUTS + n_cast:N_ATTN_INPUTS + n_cast + 2]
    cast_out = refs[N_ATTN_INPUTS + n_cast + 2:N_ATTN_INPUTS + 2 * n_cast + 2]
    (vt_ref, vtm_ref, bias_ref, ta_ref, tb_ref, ma_ref, mb_ref, m_ref, acc_ref,
     knorm_ref) = refs[N_ATTN_INPUTS + 2 * n_cast + 2:]
    for w_ref, wb_ref in zip(cast_in, cast_out):
        wb_ref[...] = w_ref[...].astype(wb_ref.dtype)

    h = pl.program_id(1)
    qi = pl.program_id(2)
    seq = k_ref.shape[0]
    n_chunks = seq // t
    assert n_chunks >= 2, "every query tile needs a neighbouring key chunk"
    acc_rows = V_DIM + SUM_ROWS

    slope = LOG2E * jnp.exp2(jnp.full((1, 1), -8.0 / N_HEADS, F32) * jnp.asarray(h + 1, F32))
    lane = lax.broadcasted_iota(jnp.int32, (1, V_DIM), 1)
    map_cols = (lax.broadcasted_iota(jnp.int32, (V_DIM, V_DIM), 0) // HEAD_DIM
                == lax.broadcasted_iota(jnp.int32, (V_DIM, V_DIM), 1)).astype(BF16)
    map_rows = (lax.broadcasted_iota(jnp.int32, (8, V_DIM), 1) // HEAD_DIM
                == lax.broadcasted_iota(jnp.int32, (8, V_DIM), 0)).astype(BF16)

    @pl.when(qi == 0)
    def _():
        for c in range(n_chunks):
            vt_ref[0:V_DIM, c * t:(c + 1) * t] = v_ref[c * t:(c + 1) * t, :].astype(F32).T.astype(BF16)
        vt_ref[V_DIM:acc_rows, :] = jnp.ones((SUM_ROWS, seq), BF16)
        vtm_ref[0:V_DIM, :] = vm_ref[...].astype(F32).T.astype(BF16)
        vtm_ref[V_DIM:acc_rows, :] = jnp.ones((SUM_ROWS, META_PAD), BF16)
        row = lax.broadcasted_iota(jnp.int32, (t, t), 0)
        col = lax.broadcasted_iota(jnp.int32, (t, t), 1)
        bias_ref[BEFORE] = slope * row.astype(F32)
        bias_ref[DIAG] = -slope * jnp.abs(col - row).astype(F32)
        bias_ref[AFTER] = -slope * row.astype(F32)
        k_sq = jnp.square(k_ref[...].astype(F32)).astype(BF16)
        sq_norms = jnp.dot(k_sq, map_cols, preferred_element_type=F32)
        knorm_ref[0:1, :] = jnp.sqrt(jnp.max(sq_norms, axis=0, keepdims=True))

    lam = (jnp.exp(jnp.sum(lq1_ref[...] * lk1_ref[...], axis=-1, keepdims=True))
           - jnp.exp(jnp.sum(lq2_ref[...] * lk2_ref[...], axis=-1, keepdims=True))
           + LAMBDA_INIT)
    meta_row = lax.broadcasted_iota(jnp.int32, (META_PAD, 1), 0)
    meta_valid = meta_row < N_META

    def attend(q, q_pos, tile_idx, out_ref):
        nq = q.shape[0]
        q32 = q.astype(F32) * (LOG2E / math.sqrt(HEAD_DIM))
        qs = (jnp.where(lane < HEAD_DIM, q32, 0.0).astype(BF16),
              jnp.where(lane >= HEAD_DIM, q32, 0.0).astype(BF16))

        def scores(k_c):
            return tuple(lax.dot_general(k_c, qc, (((1,), (1,)), ((), ())), preferred_element_type=F32)
                         for qc in qs)

        def update(c, tt, tt_max, row_bias, vt_c):
            m = m_ref[c, 0:1, 0:nq]
            m_new = jnp.maximum(m, tt_max + row_bias)
            p = jnp.exp2(tt - (m_new - row_bias))
            acc_ref[c, :, 0:nq] = (jnp.exp2(m - m_new) * acc_ref[c, :, 0:nq]
                                   + jnp.dot(vt_c, p.astype(BF16), preferred_element_type=F32))
            m_ref[c, 0:1, 0:nq] = m_new

        def produce(c_idx, kind, buf_ref, max_ref):
            k_c = k_ref[pl.ds(pl.multiple_of(c_idx * t, t), t), :]
            key_bias = bias_ref[kind, :, 0:nq]
            for c, s in enumerate(scores(k_c)):
                tt = s + key_bias
                buf_ref[c, :, 0:nq] = tt
                max_ref[c, 0:1, 0:nq] = jnp.max(tt, axis=0, keepdims=True)

        def consume(c_idx, kind, buf_ref, max_ref):
            sign = 1.0 - kind
            k_pos0 = jnp.asarray(N_META + c_idx * t, F32)
            row_bias = (-sign) * slope * (q_pos - k_pos0)
            vt_c = vt_ref[:, pl.ds(pl.multiple_of(c_idx * t, t), t)]
            for c in range(2):
                update(c, buf_ref[c, :, 0:nq], max_ref[c, 0:1, 0:nq], row_bias, vt_c)

        m_ref[...] = jnp.full(m_ref.shape, -jnp.inf, F32)
        acc_ref[...] = jnp.zeros(acc_ref.shape, F32)
        buf_a, buf_b = (ta_ref, ma_ref), (tb_ref, mb_ref)
        if tile_idx is None:
            first = (jnp.int32(0), AFTER)
        else:
            produce(tile_idx, DIAG, *buf_a)
            has_before = tile_idx > 0
            first = (jnp.where(has_before, tile_idx - 1, tile_idx + 1),
                     jnp.where(has_before, BEFORE, AFTER))

        if tile_idx is None:
            key_bias, row_bias = -slope * jnp.abs(q_pos - meta_row.astype(F32)), jnp.zeros((1, nq), F32)
        else:
            key_bias, row_bias = bias_ref[BEFORE, 0:META_PAD, 0:nq], -slope * q_pos
        for c, s in enumerate(scores(km_ref[...])):
            tt = jnp.where(meta_valid, s + key_bias, MASKED)
            update(c, tt, jnp.max(tt, axis=0, keepdims=True), row_bias, vtm_ref[...])

        produce(*first, *buf_b)
        if tile_idx is not None:
            consume(tile_idx, DIAG, *buf_a)

        q_sq = jnp.square(q32.astype(BF16).astype(F32)).astype(BF16)
        q_norm = jnp.sqrt(lax.dot_general(map_rows, q_sq, (((1,), (1,)), ((), ())),
                                          preferred_element_type=F32))
        bound = jnp.full((1, 1), -jnp.inf, F32)
        for c in range(2):
            gap = knorm_ref[0:1, c:c + 1] * q_norm[c:c + 1] * NORM_SLACK - m_ref[c, 0:1, 0:nq]
            bound = jnp.maximum(bound, jnp.max(gap, axis=-1, keepdims=True))
        reach = ((bound + SKIP_BITS) / slope - 1.0) / t + 1.0
        n_side = jnp.max(jnp.clip(jnp.floor(reach), 1.0, float(n_chunks))).astype(jnp.int32)
        if tile_idx is None:
            n_before, n_todo = 0, jnp.minimum(n_chunks, n_side)
        else:
            n_before = jnp.minimum(tile_idx, n_side)
            n_todo = n_before + jnp.minimum(n_chunks - 1 - tile_idx, n_side)

        def todo(k):
            if tile_idx is None:
                return k, AFTER
            before = k < n_before
            c_idx = jnp.where(before, tile_idx - 1 - k, tile_idx + 1 + k - n_before)
            return c_idx, jnp.where(before, BEFORE, AFTER)

        def pair(j, carry):
            k = 2 * j
            produce(*todo(k + 1), *buf_a)
            consume(*todo(k), *buf_b)
            produce(*todo(jnp.minimum(k + 2, n_todo - 1)), *buf_b)
            consume(*todo(k + 1), *buf_a)
            return carry

        lax.fori_loop(0, n_todo // 2, pair, 0)

        @pl.when(n_todo % 2 == 1)
        def _():
            consume(*todo(n_todo - 1), *buf_b)

        acc1, acc2 = acc_ref[0, :, 0:nq], acc_ref[1, :, 0:nq]
        a = acc1[0:V_DIM] / acc1[V_DIM:V_DIM + 1] - lam * (acc2[0:V_DIM] / acc2[V_DIM:V_DIM + 1])
        inv = lax.rsqrt(jnp.mean(a * a, axis=0, keepdims=True) + SUBLN_EPS)
        out_ref[...] = ((a * inv).T * subw_ref[...] * (1.0 - LAMBDA_INIT)).astype(out_ref.dtype)

    q_pos = (N_META + qi * t + lax.broadcasted_iota(jnp.int32, (1, t), 1)).astype(F32)
    attend(q_ref[...], q_pos, qi, o_ref)

    @pl.when(qi == 0)
    def _():
        pos = jnp.minimum(lax.broadcasted_iota(jnp.int32, (1, META_PAD), 1), N_META - 1)
        attend(qm_ref[...], pos.astype(F32), None, om_ref)


N_ATTN_INPUTS = 11
BF16_ROWS = 16


def _slab_rows(w, n_steps):
    n_rows = w.shape[0]
    for rows in range(BF16_ROWS, n_rows + 1, BF16_ROWS):
        if n_rows % rows == 0 and n_rows // rows <= n_steps:
            return rows
    return None


def _attention(lams, subw, y, y_meta_pad, cast_weights, *, t):
    b, seq, _ = y.shape
    n_q = seq // t
    h_blk = N_HEADS
    vec = lambda n: pl.BlockSpec((1, n), lambda bi, h, i: (0, 0))

    def slab(w):
        rows = _slab_rows(w, b * N_HEADS * n_q)
        last = w.shape[0] // rows - 1
        return pl.BlockSpec((rows, w.shape[1]),
                            lambda bi, h, i: (jnp.minimum((bi * N_HEADS + h) * n_q + i, last), 0))

    acc_rows = V_DIM + SUM_ROWS
    return pl.pallas_call(
        functools.partial(_attn_kernel, t=t, n_cast=len(cast_weights)),
        grid=(b, N_HEADS, n_q),
        in_specs=[
            vec(HEAD_DIM), vec(HEAD_DIM), vec(HEAD_DIM), vec(HEAD_DIM), vec(V_DIM),
            pl.BlockSpec((None, t, V_DIM), lambda bi, h, i: (bi, i, h)),
            pl.BlockSpec((META_PAD, V_DIM), lambda bi, h, i: (0, h)),
            pl.BlockSpec((None, seq, V_DIM), lambda bi, h, i: (bi, 0, h_blk + h)),
            pl.BlockSpec((None, seq, V_DIM), lambda bi, h, i: (bi, 0, 2 * h_blk + h)),
            pl.BlockSpec((META_PAD, V_DIM), lambda bi, h, i: (0, h_blk + h)),
            pl.BlockSpec((META_PAD, V_DIM), lambda bi, h, i: (0, 2 * h_blk + h)),
        ] + [slab(w) for w in cast_weights],
        out_specs=[
            pl.BlockSpec((None, t, V_DIM), lambda bi, h, i: (bi, i, h)),
            pl.BlockSpec((None, META_PAD, V_DIM), lambda bi, h, i: (bi, 0, h)),
        ] + [slab(w) for w in cast_weights],
        out_shape=[
            jax.ShapeDtypeStruct((b, seq, N_HEADS * V_DIM), BF16),
            jax.ShapeDtypeStruct((b, META_PAD, N_HEADS * V_DIM), BF16),
        ] + [jax.ShapeDtypeStruct(w.shape, BF16) for w in cast_weights],
        scratch_shapes=[
            pltpu.VMEM((acc_rows, seq), BF16),
            pltpu.VMEM((acc_rows, META_PAD), BF16),
            pltpu.VMEM((3, t, t), F32),
            pltpu.VMEM((2, t, t), F32),
            pltpu.VMEM((2, t, t), F32),
            pltpu.VMEM((2, 8, t), F32),
            pltpu.VMEM((2, 8, t), F32),
            pltpu.VMEM((2, 8, t), F32),
            pltpu.VMEM((2, acc_rows, t), F32),
            pltpu.VMEM((8, V_DIM), F32),
        ],
        compiler_params=_params("arbitrary", "arbitrary", "arbitrary"),
        name="diff_attention",
    )(*[v.reshape(1, HEAD_DIM) for v in lams], subw.reshape(1, V_DIM),
      y, y_meta_pad, y, y, y_meta_pad, y_meta_pad, *cast_weights)


def _chan_dft_kernel(f_ref, cs_ref, o_ref):
    t = jnp.dot(f_ref[...], cs_ref[...], preferred_element_type=F32)
    o_ref[0] = t[:, :GROUP_DIM].astype(o_ref.dtype)
    o_ref[1] = t[:, GROUP_DIM:].astype(o_ref.dtype)


def _chan_dft(y, cs, *, tr):
    b, rows, in_width = y.shape
    f_blk0 = in_width // GROUP_DIM - N_GROUPS
    return pl.pallas_call(
        _chan_dft_kernel,
        grid=(b, rows // tr, N_GROUPS),
        in_specs=[
            pl.BlockSpec((None, tr, GROUP_DIM), lambda bi, r, g: (bi, r, f_blk0 + g)),
            pl.BlockSpec((GROUP_DIM, 2 * GROUP_DIM), lambda bi, r, g: (0, 0)),
        ],
        out_specs=pl.BlockSpec((None, 2, tr, GROUP_DIM), lambda bi, r, g: (bi, 0, r, g)),
        out_shape=jax.ShapeDtypeStruct((b, 2, rows, N_GROUPS * GROUP_DIM), BF16),
        compiler_params=_params("parallel", "parallel", "parallel"),
        name="chan_dft",
    )(y, cs)


def _pos_dft_kernel(mr_ref, mm_ref, t_ref, tm_ref, nw_ref, o_ref, *, scale):
    r = jnp.dot(mr_ref[...], t_ref[...], preferred_element_type=F32)
    r = r + jnp.dot(mm_ref[...], tm_ref[...], preferred_element_type=F32)
    r = r * scale
    inv = lax.rsqrt(jnp.mean(r * r, axis=-1, keepdims=True) + NORM_EPS)
    o_ref[...] = (r * inv * nw_ref[...]).astype(o_ref.dtype)


def _pos_dft(mr, mm, t, t_meta, nw, *, tp, scale):
    p_rows = mr.shape[0]
    b, k_real, width = t.shape
    k_meta = t_meta.shape[1]
    return pl.pallas_call(
        functools.partial(_pos_dft_kernel, scale=scale),
        grid=(b, N_GROUPS, p_rows // tp),
        in_specs=[
            pl.BlockSpec((tp, k_real), lambda bi, g, p: (p, 0)),
            pl.BlockSpec((tp, k_meta), lambda bi, g, p: (p, 0)),
            pl.BlockSpec((None, k_real, GROUP_DIM), lambda bi, g, p: (bi, 0, g)),
            pl.BlockSpec((None, k_meta, GROUP_DIM), lambda bi, g, p: (0, 0, g)),
            pl.BlockSpec((1, GROUP_DIM), lambda bi, g, p: (0, g)),
        ],
        out_specs=pl.BlockSpec((None, tp, GROUP_DIM), lambda bi, g, p: (bi, p, g)),
        out_shape=jax.ShapeDtypeStruct((b, p_rows, width), BF16),
        compiler_params=_params("parallel", "parallel", "arbitrary"),
        name="pos_dft",
    )(mr, mm, t, t_meta, nw.reshape(1, width))


TABLE_STEP = 64


def _dft_tables(seq):
    l_total = N_META + seq
    c = jnp.arange(GROUP_DIM, dtype=jnp.int32)
    ang_c = ((c[:, None] * c[None, :]) % GROUP_DIM).astype(F32) * (2.0 * math.pi / GROUP_DIM)
    cs = jnp.concatenate([jnp.cos(ang_c), jnp.sin(ang_c)], axis=1).astype(BF16)

    to_rad = 2.0 * math.pi / l_total
    half_pi = jnp.asarray([0.0, 0.5 * math.pi], F32)
    meta_p = jnp.arange(N_META, dtype=jnp.int32)
    real_p = N_META + jnp.arange(seq, dtype=jnp.int32)
    real_l, real_phase = jnp.tile(real_p, 2), jnp.repeat(half_pi, seq)
    meta_l, meta_phase = jnp.tile(meta_p, 2), jnp.repeat(half_pi, N_META)

    def angle(p, l, phase):
        return ((p[:, None] * l[None, :]) % l_total).astype(F32) * to_rad + phase[None, :]

    def direct(p, l, phase):
        return jnp.cos(angle(p, l, phase)).astype(BF16)

    hi = angle(N_META + TABLE_STEP * jnp.arange(seq // TABLE_STEP, dtype=jnp.int32), real_l, 0.0 * real_phase)
    lo = angle(jnp.arange(TABLE_STEP, dtype=jnp.int32), real_l, real_phase)
    real_rows_real_cols = (jnp.cos(hi)[:, None, :] * jnp.cos(lo)[None, :, :]
                           - jnp.sin(hi)[:, None, :] * jnp.sin(lo)[None, :, :]
                           ).reshape(seq, 2 * seq).astype(BF16)
    real_rows = (real_rows_real_cols, direct(real_p, meta_l, meta_phase))
    meta_rows = (direct(meta_p, real_l, real_phase), direct(meta_p, meta_l, meta_phase))
    return cs, real_rows, meta_rows


def _matmul_res_kernel(*refs, n_pairs):
    res_ref, o_ref = refs[2 * n_pairs], refs[2 * n_pairs + 1]
    acc = res_ref[...]
    for k in range(n_pairs):
        acc = acc + jnp.dot(refs[2 * k][...], refs[2 * k + 1][...], preferred_element_type=F32)
    o_ref[...] = acc


def _matmul_res(pairs, res, *, tm, tn):
    m, n = res.shape
    in_specs, args = [], []
    for a, w, blk in pairs:
        k = a.shape[1]
        in_specs += [pl.BlockSpec((tm, k), lambda i, j: (i, 0)),
                     pl.BlockSpec((k, tn), lambda i, j, blk=blk: (blk, j))]
        args += [a, w]
    in_specs.append(pl.BlockSpec((tm, tn), lambda i, j: (i, j)))
    return pl.pallas_call(
        functools.partial(_matmul_res_kernel, n_pairs=len(pairs)),
        grid=(m // tm, n // tn),
        in_specs=in_specs,
        out_specs=pl.BlockSpec((tm, tn), lambda i, j: (i, j)),
        out_shape=jax.ShapeDtypeStruct((m, n), F32),
        compiler_params=_params("parallel", "arbitrary"),
        name="matmul_res",
    )(*args, res)


HALO_ROWS = 8
NORM_ROWS = 256


def _up_glu_kernel(x_ref, nw_ref, wg_ref, wv_ref, halo_ref, cw_ref, cb_ref, o_ref, hn_ref, *, tm):
    @pl.when(pl.program_id(1) == 0)
    def _():
        def norm_rows(r, carry):
            rows = pl.ds(pl.multiple_of(r * NORM_ROWS, NORM_ROWS), NORM_ROWS)
            hn_ref[rows, :] = _norm_rows(x_ref, nw_ref, rows)
            return carry
        lax.fori_loop(0, tm // NORM_ROWS, norm_rows, 0)

    hn = hn_ref[...]
    g = jnp.dot(hn, wg_ref[...], preferred_element_type=F32)
    v = jnp.dot(hn, wv_ref[...], preferred_element_type=F32)
    row = lax.broadcasted_iota(jnp.int32, g.shape, 0)
    g_prev = jnp.where(row == 0, halo_ref[0, 0:1, :], pltpu.roll(g, 1, axis=0))
    g_next = jnp.where(row == tm - 1, halo_ref[0, 1:2, :], pltpu.roll(g, tm - 1, axis=0))
    conv = g_prev * cw_ref[0:1, :] + g * cw_ref[1:2, :] + g_next * cw_ref[2:3, :] + cb_ref[...]
    gelu = 0.5 * conv * (1.0 + lax.erf(conv * math.sqrt(0.5)))
    o_ref[...] = (gelu * v).astype(o_ref.dtype)


def _up_glu(x, nw, w_up, halo, conv_w, conv_b, *, tm, tc):
    m, d = x.shape
    d_ff = w_up.shape[1] // 2
    assert d_ff % LANES == 0 and d_ff >= tc
    n_c = pl.cdiv(d_ff, tc)
    el = pl.Element

    def col(c, base=0):
        return pl.multiple_of(base + jnp.minimum(c * tc, d_ff - tc), LANES)

    return pl.pallas_call(
        functools.partial(_up_glu_kernel, tm=tm),
        grid=(m // tm, n_c),
        in_specs=[
            pl.BlockSpec((tm, d), lambda i, c: (i, 0), pipeline_mode=pl.Buffered(1)),
            pl.BlockSpec((1, d), lambda i, c: (0, 0)),
            pl.BlockSpec((el(d), el(tc)), lambda i, c: (0, col(c))),
            pl.BlockSpec((el(d), el(tc)), lambda i, c: (0, col(c, d_ff))),
            pl.BlockSpec((el(1), el(HALO_ROWS), el(tc)), lambda i, c: (i, 0, col(c))),
            pl.BlockSpec((el(CONV_WIDTH), el(tc)), lambda i, c: (0, col(c))),
            pl.BlockSpec((el(1), el(tc)), lambda i, c: (0, col(c))),
        ],
        out_specs=pl.BlockSpec((el(tm), el(tc)), lambda i, c: (pl.multiple_of(i * tm, tm), col(c))),
        out_shape=jax.ShapeDtypeStruct((m, d_ff), BF16),
        scratch_shapes=[pltpu.VMEM((tm, d), BF16)],
        compiler_params=_params("parallel", "arbitrary"),
        name="up_glu",
    )(x, nw.reshape(1, d), w_up, w_up, halo, conv_w, conv_b.reshape(1, d_ff))


def _gate_halo(h_mid, h_mid_meta, nw, w_up, *, b, seq, tm, d_ff):
    d = h_mid.shape[-1]
    n_t = seq // tm
    h3 = h_mid.reshape(b, seq, d)
    pairs = [h3[:, k * tm - 1:k * tm + 1] for k in range(1, n_t)]
    meta_last = h_mid_meta.reshape(b, N_META, d)[:, N_META - 1:]
    rows = jnp.concatenate([meta_last] + pairs, axis=1).reshape(-1, d)
    n_rows = rows.shape[0]
    pad = -n_rows % 16
    rows = jnp.pad(rows, ((0, pad), (0, 0)))
    gates = _norm_matmul(rows, nw, w_up, tm=n_rows + pad, tn=256, n_cols=d_ff)[:n_rows]
    gates = gates.astype(F32).reshape(b, 2 * n_t - 1, d_ff)
    before = jnp.concatenate([gates[:, 0:1], gates[:, 1::2]], axis=1)
    after = jnp.concatenate([gates[:, 2::2], jnp.zeros((b, 1, d_ff), F32)], axis=1)
    halo = jnp.stack([before, after], axis=2)
    halo = jnp.pad(halo, ((0, 0), (0, 0), (0, HALO_ROWS - 2), (0, 0)))
    return halo.reshape(b * n_t, HALO_ROWS, d_ff)


def _rmsnorm_kernel(x_ref, w_ref, o_ref):
    x = x_ref[...]
    inv = lax.rsqrt(jnp.mean(x * x, axis=-1, keepdims=True) + NORM_EPS)
    o_ref[...] = x * inv * w_ref[...]


def _rmsnorm(x, w, *, tm):
    m, d = x.shape
    return pl.pallas_call(
        _rmsnorm_kernel,
        grid=(m // tm,),
        in_specs=[pl.BlockSpec((tm, d), lambda i: (i, 0)), pl.BlockSpec((1, d), lambda i: (0, 0))],
        out_specs=pl.BlockSpec((tm, d), lambda i: (i, 0)),
        out_shape=jax.ShapeDtypeStruct((m, d), F32),
        compiler_params=_params("parallel"),
        name="final_rmsnorm",
    )(x, w.reshape(1, d))


def kernel(x, meta_tokens, norm1_w, w_in, lambda_q1, lambda_k1, lambda_q2, lambda_k2,
           diff_subln_w, fourier_norm_w, w_out, norm2_w, w_up, conv_w, conv_b, w_down,
           final_norm_w):
    b, seq, d = x.shape
    m = b * seq
    in_width = w_in.shape[-1]
    d_ff = w_down.shape[1]
    assert w_in.shape[0] == 1 and meta_tokens.shape[0] == N_META

    w_in_b = w_in[0].astype(BF16)
    lams = (lambda_q1[0], lambda_k1[0], lambda_q2[0], lambda_k2[0])
    t_attn = min(512, seq)

    later = (w_out[0], w_up[0], w_down[0])
    in_attention = all(_slab_rows(w, b * N_HEADS * (seq // t_attn)) for w in later)

    y = _norm_matmul(x.reshape(m, d), norm1_w[0], w_in_b, tm=512, tn=1024).reshape(b, seq, in_width)
    y_meta = _norm_matmul(meta_tokens, norm1_w[0], w_in_b, tm=N_META, tn=1024)

    y_meta_pad = jnp.pad(y_meta, ((0, META_PAD - N_META), (0, 0)))
    attn, attn_meta, *cast = _attention(lams, diff_subln_w[0], y, y_meta_pad,
                                        later if in_attention else (), t=t_attn)
    attn_meta = attn_meta[:, :N_META]
    w_out_b, w_up_b, w_down_b = cast if in_attention else (w.astype(BF16) for w in later)

    cs, real_rows, meta_rows = _dft_tables(seq)
    four_w = N_GROUPS * GROUP_DIM
    t = _chan_dft(y, cs, tr=min(1024, seq)).reshape(b, 2 * seq, four_w)
    t_meta = _chan_dft(y_meta[None], cs, tr=N_META).reshape(1, 2 * N_META, four_w)
    scale = 1.0 / math.sqrt((N_META + seq) * GROUP_DIM)
    four = _pos_dft(*real_rows, t, t_meta, fourier_norm_w[0], tp=512, scale=scale)
    four_meta = _pos_dft(*meta_rows, t, t_meta, fourier_norm_w[0], tp=N_META, scale=scale)

    attn_w = attn.shape[-1]
    h_mid = _matmul_res([(attn.reshape(m, attn_w), w_out_b, 0), (four.reshape(m, four_w), w_out_b, 1)],
                        x.reshape(m, d), tm=512, tn=min(1024, d))
    meta_res = jnp.broadcast_to(meta_tokens[None], (b, N_META, d)).reshape(b * N_META, d)
    h_mid_meta = _matmul_res([(attn_meta.reshape(b * N_META, attn_w), w_out_b, 0),
                              (four_meta.reshape(b * N_META, four_w), w_out_b, 1)],
                             meta_res, tm=b * N_META, tn=min(1024, d))

    tm_up = min(1024, seq)
    halo = _gate_halo(h_mid, h_mid_meta, norm2_w[0], w_up_b, b=b, seq=seq, tm=tm_up, d_ff=d_ff)
    z = _up_glu(h_mid, norm2_w[0], w_up_b, halo, conv_w[0], conv_b[0], tm=tm_up, tc=512)
    h_out = _matmul_res([(z, w_down_b, 0)], h_mid, tm=512, tn=min(512, d))

    return _rmsnorm(h_out, final_norm_w, tm=512).reshape(b, seq, d)
```

```python
import functools
import math

import jax
import jax.numpy as jnp
from jax import lax
from jax.experimental import pallas as pl
from jax.experimental.pallas import tpu as pltpu

N_META = 16
N_HEADS = 16
HEAD_DIM = 64
V_DIM = 128
LANES = 128
N_GROUPS = 4
GROUP_DIM = 512
CONV_WIDTH = 3
NORM_EPS = 1e-6
SUBLN_EPS = 1e-5
LAMBDA_INIT = 0.8 - 0.6 * math.exp(-0.3 * 0)

VMEM_LIMIT_BYTES = 56 * 1024 * 1024
BF16 = jnp.bfloat16
F32 = jnp.float32


def _params(*sem):
    return pltpu.CompilerParams(dimension_semantics=sem, vmem_limit_bytes=VMEM_LIMIT_BYTES)


def _norm_rows(x_ref, nw_ref, rows):
    x = x_ref[rows, :]
    inv = lax.rsqrt(jnp.mean(x * x, axis=-1, keepdims=True) + NORM_EPS)
    return (x * inv * nw_ref[...]).astype(BF16)


def _norm_matmul_kernel(x_ref, nw_ref, w_ref, o_ref, hn_ref, *, row_chunks):
    @pl.when(pl.program_id(1) == 0)
    def _():
        for rows in row_chunks:
            hn = _norm_rows(x_ref, nw_ref, rows)
            hn_ref[rows, :] = hn
            o_ref[rows, :] = jnp.dot(hn, w_ref[...], preferred_element_type=F32).astype(o_ref.dtype)

    @pl.when(pl.program_id(1) != 0)
    def _():
        o_ref[...] = jnp.dot(hn_ref[...], w_ref[...], preferred_element_type=F32).astype(o_ref.dtype)


def _norm_matmul(x, nw, w, *, tm, tn, n_cols=None):
    m, d = x.shape
    n = w.shape[1] if n_cols is None else n_cols
    step = min(tm, NORM_ROWS)
    row_chunks = tuple(slice(r, r + step) for r in range(0, tm, step))
    return pl.pallas_call(
        functools.partial(_norm_matmul_kernel, row_chunks=row_chunks),
        grid=(m // tm, n // tn),
        in_specs=[
            pl.BlockSpec((tm, d), lambda i, j: (i, 0)),
            pl.BlockSpec((1, d), lambda i, j: (0, 0)),
            pl.BlockSpec((d, tn), lambda i, j: (0, j)),
        ],
        out_specs=pl.BlockSpec((tm, tn), lambda i, j: (i, j)),
        out_shape=jax.ShapeDtypeStruct((m, n), BF16),
        scratch_shapes=[pltpu.VMEM((tm, d), BF16)],
        compiler_params=_params("parallel", "arbitrary"),
        name="norm_matmul",
    )(x, nw.reshape(1, d), w)


META_PAD = 128
SUM_ROWS = 16
LOG2E = 1.0 / math.log(2.0)
BEFORE, DIAG, AFTER = 0, 1, 2
SKIP_BITS = 160.0
NORM_SLACK = 1.01


def _attn_kernel(*refs, t, n_cast):
    (lq1_ref, lk1_ref, lq2_ref, lk2_ref, subw_ref, q_ref, qm_ref, k_ref, v_ref, km_ref,
     vm_ref) = refs[:N_ATTN_INPUTS]
    cast_in = refs[N_ATTN_INPUTS:N_ATTN_INPUTS + n_cast]
    o_ref, om_ref = refs[N_ATTN_INPUTS + n_cast:N_ATTN_INPUTS + n_cast + 2]
    cast_out = refs[N_ATTN_INPUTS + n_cast + 2:N_ATTN_INPUTS + 2 * n_cast + 2]
    (vt_ref, vtm_ref, bias_ref, ta_ref, tb_ref, ma_ref, mb_ref, m_ref, acc_ref,
     knorm_ref) = refs[N_ATTN_INPUTS + 2 * n_cast + 2:]
    for w_ref, wb_ref in zip(cast_in, cast_out):
        wb_ref[...] = w_ref[...].astype(wb_ref.dtype)

    h = pl.program_id(1)
    qi = pl.program_id(2)
    seq = k_ref.shape[0]
    n_chunks = seq // t
    assert n_chunks >= 2, "every query tile needs a neighbouring key chunk"
    acc_rows = V_DIM + SUM_ROWS

    slope = LOG2E * jnp.exp2(jnp.full((1, 1), -8.0 / N_HEADS, F32) * jnp.asarray(h + 1, F32))
    lane = lax.broadcasted_iota(jnp.int32, (1, V_DIM), 1)
    map_cols = (lax.broadcasted_iota(jnp.int32, (V_DIM, V_DIM), 0) // HEAD_DIM
                == lax.broadcasted_iota(jnp.int32, (V_DIM, V_DIM), 1)).astype(BF16)
    map_rows = (lax.broadcasted_iota(jnp.int32, (8, V_DIM), 1) // HEAD_DIM
                == lax.broadcasted_iota(jnp.int32, (8, V_DIM), 0)).astype(BF16)

    @pl.when(qi == 0)
    def _():
        for c in range(n_chunks):
            vt_ref[0:V_DIM, c * t:(c + 1) * t] = v_ref[c * t:(c + 1) * t, :].astype(F32).T.astype(BF16)
        vt_ref[V_DIM:acc_rows, :] = jnp.ones((SUM_ROWS, seq), BF16)
        vtm_ref[0:V_DIM, :] = vm_ref[...].astype(F32).T.astype(BF16)
        vtm_ref[V_DIM:acc_rows, :] = jnp.ones((SUM_ROWS, META_PAD), BF16)
        row = lax.broadcasted_iota(jnp.int32, (t, t), 0)
        col = lax.broadcasted_iota(jnp.int32, (t, t), 1)
        bias_ref[BEFORE] = slope * row.astype(F32)
        bias_ref[DIAG] = -slope * jnp.abs(col - row).astype(F32)
        bias_ref[AFTER] = -slope * row.astype(F32)
        k_sq = jnp.square(k_ref[...].astype(F32)).astype(BF16)
        sq_norms = jnp.dot(k_sq, map_cols, preferred_element_type=F32)
        knorm_ref[0:1, :] = jnp.sqrt(jnp.max(sq_norms, axis=0, keepdims=True))

    lam = (jnp.exp(jnp.sum(lq1_ref[...] * lk1_ref[...], axis=-1, keepdims=True))
           - jnp.exp(jnp.sum(lq2_ref[...] * lk2_ref[...], axis=-1, keepdims=True))
           + LAMBDA_INIT)
    meta_row = lax.broadcasted_iota(jnp.int32, (N_META, 1), 0)

    def attend(q, q_pos, tile_idx, out_ref):
        nq = q.shape[0]
        q32 = q.astype(F32) * (LOG2E / math.sqrt(HEAD_DIM))
        qs = (jnp.where(lane < HEAD_DIM, q32, 0.0).astype(BF16),
              jnp.where(lane >= HEAD_DIM, q32, 0.0).astype(BF16))

        def scores(k_c):
            return tuple(lax.dot_general(k_c, qc, (((1,), (1,)), ((), ())), preferred_element_type=F32)
                         for qc in qs)

        def update(c, tt, tt_max, row_bias, vt_c):
            m = m_ref[c, 0:1, 0:nq]
            m_new = jnp.maximum(m, tt_max + row_bias)
            p = jnp.exp2(tt - (m_new - row_bias))
            acc_ref[c, :, 0:nq] = (jnp.exp2(m - m_new) * acc_ref[c, :, 0:nq]
                                   + jnp.dot(vt_c, p.astype(BF16), preferred_element_type=F32))
            m_ref[c, 0:1, 0:nq] = m_new

        def produce(c_idx, kind, buf_ref, max_ref):
            k_c = k_ref[pl.ds(pl.multiple_of(c_idx * t, t), t), :]
            key_bias = bias_ref[kind, :, 0:nq]
            for c, s in enumerate(scores(k_c)):
                tt = s + key_bias
                buf_ref[c, :, 0:nq] = tt
                max_ref[c, 0:1, 0:nq] = jnp.max(tt, axis=0, keepdims=True)

        def consume(c_idx, kind, buf_ref, max_ref):
            sign = 1.0 - kind
            k_pos0 = jnp.asarray(N_META + c_idx * t, F32)
            row_bias = (-sign) * slope * (q_pos - k_pos0)
            vt_c = vt_ref[:, pl.ds(pl.multiple_of(c_idx * t, t), t)]
            for c in range(2):
                update(c, buf_ref[c, :, 0:nq], max_ref[c, 0:1, 0:nq], row_bias, vt_c)

        m_ref[...] = jnp.full(m_ref.shape, -jnp.inf, F32)
        acc_ref[...] = jnp.zeros(acc_ref.shape, F32)
        buf_a, buf_b = (ta_ref, ma_ref), (tb_ref, mb_ref)
        if tile_idx is None:
            first = (jnp.int32(0), AFTER)
        else:
            produce(tile_idx, DIAG, *buf_a)
            has_before = tile_idx > 0
            first = (jnp.where(has_before, tile_idx - 1, tile_idx + 1),
                     jnp.where(has_before, BEFORE, AFTER))

        if tile_idx is None:
            key_bias, row_bias = -slope * jnp.abs(q_pos - meta_row.astype(F32)), jnp.zeros((1, nq), F32)
        else:
            key_bias, row_bias = bias_ref[BEFORE, 0:N_META, 0:nq], -slope * q_pos
        for c, s in enumerate(scores(km_ref[0:N_META, :])):
            tt = s + key_bias
            update(c, tt, jnp.max(tt, axis=0, keepdims=True), row_bias, vtm_ref[:, 0:N_META])

        produce(*first, *buf_b)
        if tile_idx is not None:
            consume(tile_idx, DIAG, *buf_a)

        q_sq = jnp.square(q32.astype(BF16).astype(F32)).astype(BF16)
        q_norm = jnp.sqrt(lax.dot_general(map_rows, q_sq, (((1,), (1,)), ((), ())),
                                          preferred_element_type=F32))
        bound = jnp.full((1, 1), -jnp.inf, F32)
        for c in range(2):
            gap = knorm_ref[0:1, c:c + 1] * q_norm[c:c + 1] * NORM_SLACK - m_ref[c, 0:1, 0:nq]
            bound = jnp.maximum(bound, jnp.max(gap, axis=-1, keepdims=True))
        reach = ((bound + SKIP_BITS) / slope - 1.0) / t + 1.0
        n_side = jnp.max(jnp.clip(jnp.floor(reach), 1.0, float(n_chunks))).astype(jnp.int32)
        if tile_idx is None:
            n_before, n_todo = 0, jnp.minimum(n_chunks, n_side)
        else:
            n_before = jnp.minimum(tile_idx, n_side)
            n_todo = n_before + jnp.minimum(n_chunks - 1 - tile_idx, n_side)

        def todo(k):
            if tile_idx is None:
                return k, AFTER
            before = k < n_before
            c_idx = jnp.where(before, tile_idx - 1 - k, tile_idx + 1 + k - n_before)
            return c_idx, jnp.where(before, BEFORE, AFTER)

        def pair(j, carry):
            k = 2 * j
            produce(*todo(k + 1), *buf_a)
            consume(*todo(k), *buf_b)
            produce(*todo(k + 2), *buf_b)
            consume(*todo(k + 1), *buf_a)
            return carry

        n_pairs = (n_todo - 1) // 2
        lax.fori_loop(0, n_pairs, pair, 0)

        @pl.when(n_todo - 2 * n_pairs == 2)
        def _():
            produce(*todo(n_todo - 1), *buf_a)
            consume(*todo(n_todo - 2), *buf_b)
            consume(*todo(n_todo - 1), *buf_a)

        @pl.when(n_todo - 2 * n_pairs == 1)
        def _():
            consume(*todo(n_todo - 1), *buf_b)

        acc1, acc2 = acc_ref[0, :, 0:nq], acc_ref[1, :, 0:nq]
        a = (acc1[0:V_DIM] * (1.0 / acc1[V_DIM:V_DIM + 1])
             - acc2[0:V_DIM] * (lam / acc2[V_DIM:V_DIM + 1]))
        inv = lax.rsqrt(jnp.mean(a * a, axis=0, keepdims=True) + SUBLN_EPS)
        out_ref[...] = ((a * inv).T * subw_ref[...] * (1.0 - LAMBDA_INIT)).astype(out_ref.dtype)

    q_pos = (N_META + qi * t + lax.broadcasted_iota(jnp.int32, (1, t), 1)).astype(F32)
    attend(q_ref[...], q_pos, qi, o_ref)

    @pl.when(qi == 0)
    def _():
        pos = jnp.minimum(lax.broadcasted_iota(jnp.int32, (1, META_PAD), 1), N_META - 1)
        attend(qm_ref[...], pos.astype(F32), None, om_ref)


N_ATTN_INPUTS = 11
BF16_ROWS = 16


def _slab_rows(w, n_steps):
    n_rows = w.shape[0]
    for rows in range(BF16_ROWS, n_rows + 1, BF16_ROWS):
        if n_rows % rows == 0 and n_rows // rows <= n_steps:
            return rows
    return None


def _attention(lams, subw, y, y_meta_pad, cast_weights, *, t):
    b, seq, _ = y.shape
    n_q = seq // t
    h_blk = N_HEADS
    vec = lambda n: pl.BlockSpec((1, n), lambda bi, h, i: (0, 0))

    def slab(w):
        rows = _slab_rows(w, b * N_HEADS * n_q)
        last = w.shape[0] // rows - 1
        return pl.BlockSpec((rows, w.shape[1]),
                            lambda bi, h, i: (jnp.minimum((bi * N_HEADS + h) * n_q + i, last), 0))

    acc_rows = V_DIM + SUM_ROWS
    return pl.pallas_call(
        functools.partial(_attn_kernel, t=t, n_cast=len(cast_weights)),
        grid=(b, N_HEADS, n_q),
        in_specs=[
            vec(HEAD_DIM), vec(HEAD_DIM), vec(HEAD_DIM), vec(HEAD_DIM), vec(V_DIM),
            pl.BlockSpec((None, t, V_DIM), lambda bi, h, i: (bi, i, h)),
            pl.BlockSpec((META_PAD, V_DIM), lambda bi, h, i: (0, h)),
            pl.BlockSpec((None, seq, V_DIM), lambda bi, h, i: (bi, 0, h_blk + h)),
            pl.BlockSpec((None, seq, V_DIM), lambda bi, h, i: (bi, 0, 2 * h_blk + h)),
            pl.BlockSpec((META_PAD, V_DIM), lambda bi, h, i: (0, h_blk + h)),
            pl.BlockSpec((META_PAD, V_DIM), lambda bi, h, i: (0, 2 * h_blk + h)),
        ] + [slab(w) for w in cast_weights],
        out_specs=[
            pl.BlockSpec((None, t, V_DIM), lambda bi, h, i: (bi, i, h)),
            pl.BlockSpec((None, META_PAD, V_DIM), lambda bi, h, i: (bi, 0, h)),
        ] + [slab(w) for w in cast_weights],
        out_shape=[
            jax.ShapeDtypeStruct((b, seq, N_HEADS * V_DIM), BF16),
            jax.ShapeDtypeStruct((b, META_PAD, N_HEADS * V_DIM), BF16),
        ] + [jax.ShapeDtypeStruct(w.shape, BF16) for w in cast_weights],
        scratch_shapes=[
            pltpu.VMEM((acc_rows, seq), BF16),
            pltpu.VMEM((acc_rows, META_PAD), BF16),
            pltpu.VMEM((3, t, t), F32),
            pltpu.VMEM((2, t, t), F32),
            pltpu.VMEM((2, t, t), F32),
            pltpu.VMEM((2, 8, t), F32),
            pltpu.VMEM((2, 8, t), F32),
            pltpu.VMEM((2, 8, t), F32),
            pltpu.VMEM((2, acc_rows, t), F32),
            pltpu.VMEM((8, V_DIM), F32),
        ],
        compiler_params=_params("arbitrary", "arbitrary", "arbitrary"),
        name="diff_attention",
    )(*[v.reshape(1, HEAD_DIM) for v in lams], subw.reshape(1, V_DIM),
      y, y_meta_pad, y, y, y_meta_pad, y_meta_pad, *cast_weights)


def _chan_dft_kernel(f_ref, cs_ref, o_ref):
    t = jnp.dot(f_ref[...], cs_ref[...], preferred_element_type=F32)
    o_ref[0] = t[:, :GROUP_DIM].astype(o_ref.dtype)
    o_ref[1] = t[:, GROUP_DIM:].astype(o_ref.dtype)


def _chan_dft(y, cs, *, tr):
    b, rows, in_width = y.shape
    f_blk0 = in_width // GROUP_DIM - N_GROUPS
    return pl.pallas_call(
        _chan_dft_kernel,
        grid=(b, rows // tr, N_GROUPS),
        in_specs=[
            pl.BlockSpec((None, tr, GROUP_DIM), lambda bi, r, g: (bi, r, f_blk0 + g)),
            pl.BlockSpec((GROUP_DIM, 2 * GROUP_DIM), lambda bi, r, g: (0, 0)),
        ],
        out_specs=pl.BlockSpec((None, 2, tr, GROUP_DIM), lambda bi, r, g: (bi, 0, r, g)),
        out_shape=jax.ShapeDtypeStruct((b, 2, rows, N_GROUPS * GROUP_DIM), BF16),
        compiler_params=_params("parallel", "parallel", "parallel"),
        name="chan_dft",
    )(y, cs)


def _pos_dft_kernel(mr_ref, mm_ref, t_ref, tm_ref, nw_ref, o_ref, *, scale):
    r = jnp.dot(mr_ref[...], t_ref[...], preferred_element_type=F32)
    r = r + jnp.dot(mm_ref[...], tm_ref[...], preferred_element_type=F32)
    r = r * scale
    inv = lax.rsqrt(jnp.mean(r * r, axis=-1, keepdims=True) + NORM_EPS)
    o_ref[...] = (r * inv * nw_ref[...]).astype(o_ref.dtype)


def _pos_dft(mr, mm, t, t_meta, nw, *, tp, scale):
    p_rows = mr.shape[0]
    b, k_real, width = t.shape
    k_meta = t_meta.shape[1]
    return pl.pallas_call(
        functools.partial(_pos_dft_kernel, scale=scale),
        grid=(b, N_GROUPS, p_rows // tp),
        in_specs=[
            pl.BlockSpec((tp, k_real), lambda bi, g, p: (p, 0)),
            pl.BlockSpec((tp, k_meta), lambda bi, g, p: (p, 0)),
            pl.BlockSpec((None, k_real, GROUP_DIM), lambda bi, g, p: (bi, 0, g)),
            pl.BlockSpec((None, k_meta, GROUP_DIM), lambda bi, g, p: (0, 0, g)),
            pl.BlockSpec((1, GROUP_DIM), lambda bi, g, p: (0, g)),
        ],
        out_specs=pl.BlockSpec((None, tp, GROUP_DIM), lambda bi, g, p: (bi, p, g)),
        out_shape=jax.ShapeDtypeStruct((b, p_rows, width), BF16),
        compiler_params=_params("parallel", "parallel", "arbitrary"),
        name="pos_dft",
    )(mr, mm, t, t_meta, nw.reshape(1, width))


TABLE_STEP = 64


def _dft_tables(seq):
    l_total = N_META + seq
    c = jnp.arange(GROUP_DIM, dtype=jnp.int32)
    ang_c = ((c[:, None] * c[None, :]) % GROUP_DIM).astype(F32) * (2.0 * math.pi / GROUP_DIM)
    cs = jnp.concatenate([jnp.cos(ang_c), jnp.sin(ang_c)], axis=1).astype(BF16)

    to_rad = 2.0 * math.pi / l_total
    half_pi = jnp.asarray([0.0, 0.5 * math.pi], F32)
    meta_p = jnp.arange(N_META, dtype=jnp.int32)
    real_p = N_META + jnp.arange(seq, dtype=jnp.int32)
    real_l, real_phase = jnp.tile(real_p, 2), jnp.repeat(half_pi, seq)
    meta_l, meta_phase = jnp.tile(meta_p, 2), jnp.repeat(half_pi, N_META)

    def angle(p, l, phase):
        return ((p[:, None] * l[None, :]) % l_total).astype(F32) * to_rad + phase[None, :]

    def direct(p, l, phase):
        return jnp.cos(angle(p, l, phase)).astype(BF16)

    hi = angle(N_META + TABLE_STEP * jnp.arange(seq // TABLE_STEP, dtype=jnp.int32), real_l, 0.0 * real_phase)
    lo = angle(jnp.arange(TABLE_STEP, dtype=jnp.int32), real_l, real_phase)
    real_rows_real_cols = (jnp.cos(hi)[:, None, :] * jnp.cos(lo)[None, :, :]
                           - jnp.sin(hi)[:, None, :] * jnp.sin(lo)[None, :, :]
                           ).reshape(seq, 2 * seq).astype(BF16)
    real_rows = (real_rows_real_cols, direct(real_p, meta_l, meta_phase))
    meta_rows = (direct(meta_p, real_l, real_phase), direct(meta_p, meta_l, meta_phase))
    return cs, real_rows, meta_rows


def _matmul_res_kernel(*refs, n_pairs):
    res_ref, o_ref = refs[2 * n_pairs], refs[2 * n_pairs + 1]
    acc = res_ref[...]
    for k in range(n_pairs):
        acc = acc + jnp.dot(refs[2 * k][...], refs[2 * k + 1][...], preferred_element_type=F32)
    o_ref[...] = acc


def _matmul_res(pairs, res, *, tm, tn):
    m, n = res.shape
    in_specs, args = [], []
    for a, w, blk in pairs:
        k = a.shape[1]
        in_specs += [pl.BlockSpec((tm, k), lambda i, j: (i, 0)),
                     pl.BlockSpec((k, tn), lambda i, j, blk=blk: (blk, j))]
        args += [a, w]
    in_specs.append(pl.BlockSpec((tm, tn), lambda i, j: (i, j)))
    return pl.pallas_call(
        functools.partial(_matmul_res_kernel, n_pairs=len(pairs)),
        grid=(m // tm, n // tn),
        in_specs=in_specs,
        out_specs=pl.BlockSpec((tm, tn), lambda i, j: (i, j)),
        out_shape=jax.ShapeDtypeStruct((m, n), F32),
        compiler_params=_params("parallel", "arbitrary"),
        name="matmul_res",
    )(*args, res)


HALO_ROWS = 8
NORM_ROWS = 256


def _up_glu_kernel(x_ref, nw_ref, wg_ref, wv_ref, halo_ref, cw_ref, cb_ref, o_ref, hn_ref, *, tm):
    @pl.when(pl.program_id(1) == 0)
    def _():
        def norm_rows(r, carry):
            rows = pl.ds(pl.multiple_of(r * NORM_ROWS, NORM_ROWS), NORM_ROWS)
            hn_ref[rows, :] = _norm_rows(x_ref, nw_ref, rows)
            return carry
        lax.fori_loop(0, tm // NORM_ROWS, norm_rows, 0)

    hn = hn_ref[...]
    g = jnp.dot(hn, wg_ref[...], preferred_element_type=F32)
    v = jnp.dot(hn, wv_ref[...], preferred_element_type=F32)
    row = lax.broadcasted_iota(jnp.int32, g.shape, 0)
    g_prev = jnp.where(row == 0, halo_ref[0, 0:1, :], pltpu.roll(g, 1, axis=0))
    g_next = jnp.where(row == tm - 1, halo_ref[0, 1:2, :], pltpu.roll(g, tm - 1, axis=0))
    conv = g_prev * cw_ref[0:1, :] + g * cw_ref[1:2, :] + g_next * cw_ref[2:3, :] + cb_ref[...]
    gelu = 0.5 * conv * (1.0 + lax.erf(conv * math.sqrt(0.5)))
    o_ref[...] = (gelu * v).astype(o_ref.dtype)


def _up_glu(x, nw, w_up, halo, conv_w, conv_b, *, tm, tc):
    m, d = x.shape
    d_ff = w_up.shape[1] // 2
    assert d_ff % LANES == 0 and d_ff >= tc
    n_c = pl.cdiv(d_ff, tc)
    el = pl.Element

    def col(c, base=0):
        return pl.multiple_of(base + jnp.minimum(c * tc, d_ff - tc), LANES)

    return pl.pallas_call(
        functools.partial(_up_glu_kernel, tm=tm),
        grid=(m // tm, n_c),
        in_specs=[
            pl.BlockSpec((tm, d), lambda i, c: (i, 0), pipeline_mode=pl.Buffered(1)),
            pl.BlockSpec((1, d), lambda i, c: (0, 0)),
            pl.BlockSpec((el(d), el(tc)), lambda i, c: (0, col(c))),
            pl.BlockSpec((el(d), el(tc)), lambda i, c: (0, col(c, d_ff))),
            pl.BlockSpec((el(1), el(HALO_ROWS), el(tc)), lambda i, c: (i, 0, col(c))),
            pl.BlockSpec((el(CONV_WIDTH), el(tc)), lambda i, c: (0, col(c))),
            pl.BlockSpec((el(1), el(tc)), lambda i, c: (0, col(c))),
        ],
        out_specs=pl.BlockSpec((el(tm), el(tc)), lambda i, c: (pl.multiple_of(i * tm, tm), col(c))),
        out_shape=jax.ShapeDtypeStruct((m, d_ff), BF16),
        scratch_shapes=[pltpu.VMEM((tm, d), BF16)],
        compiler_params=_params("parallel", "arbitrary"),
        name="up_glu",
    )(x, nw.reshape(1, d), w_up, w_up, halo, conv_w, conv_b.reshape(1, d_ff))


def _gate_halo(h_mid, h_mid_meta, nw, w_up, *, b, seq, tm, d_ff):
    d = h_mid.shape[-1]
    n_t = seq // tm
    h3 = h_mid.reshape(b, seq, d)
    pairs = [h3[:, k * tm - 1:k * tm + 1] for k in range(1, n_t)]
    meta_last = h_mid_meta.reshape(b, N_META, d)[:, N_META - 1:]
    rows = jnp.concatenate([meta_last] + pairs, axis=1).reshape(-1, d)
    n_rows = rows.shape[0]
    pad = -n_rows % 16
    rows = jnp.pad(rows, ((0, pad), (0, 0)))
    gates = _norm_matmul(rows, nw, w_up, tm=n_rows + pad, tn=256, n_cols=d_ff)[:n_rows]
    gates = gates.astype(F32).reshape(b, 2 * n_t - 1, d_ff)
    before = jnp.concatenate([gates[:, 0:1], gates[:, 1::2]], axis=1)
    after = jnp.concatenate([gates[:, 2::2], jnp.zeros((b, 1, d_ff), F32)], axis=1)
    halo = jnp.stack([before, after], axis=2)
    halo = jnp.pad(halo, ((0, 0), (0, 0), (0, HALO_ROWS - 2), (0, 0)))
    return halo.reshape(b * n_t, HALO_ROWS, d_ff)


def _rmsnorm_kernel(x_ref, w_ref, o_ref):
    x = x_ref[...]
    inv = lax.rsqrt(jnp.mean(x * x, axis=-1, keepdims=True) + NORM_EPS)
    o_ref[...] = x * inv * w_ref[...]


def _rmsnorm(x, w, *, tm):
    m, d = x.shape
    return pl.pallas_call(
        _rmsnorm_kernel,
        grid=(m // tm,),
        in_specs=[pl.BlockSpec((tm, d), lambda i: (i, 0)), pl.BlockSpec((1, d), lambda i: (0, 0))],
        out_specs=pl.BlockSpec((tm, d), lambda i: (i, 0)),
        out_shape=jax.ShapeDtypeStruct((m, d), F32),
        compiler_params=_params("parallel"),
        name="final_rmsnorm",
    )(x, w.reshape(1, d))


def kernel(x, meta_tokens, norm1_w, w_in, lambda_q1, lambda_k1, lambda_q2, lambda_k2,
           diff_subln_w, fourier_norm_w, w_out, norm2_w, w_up, conv_w, conv_b, w_down,
           final_norm_w):
    b, seq, d = x.shape
    m = b * seq
    in_width = w_in.shape[-1]
    d_ff = w_down.shape[1]
    assert w_in.shape[0] == 1 and meta_tokens.shape[0] == N_META

    w_in_b = w_in[0].astype(BF16)
    lams = (lambda_q1[0], lambda_k1[0], lambda_q2[0], lambda_k2[0])
    t_attn = min(512, seq)

    later = (w_out[0], w_up[0], w_down[0])
    in_attention = all(_slab_rows(w, b * N_HEADS * (seq // t_attn)) for w in later)

    y = _norm_matmul(x.reshape(m, d), norm1_w[0], w_in_b, tm=512, tn=1024).reshape(b, seq, in_width)
    y_meta = _norm_matmul(meta_tokens, norm1_w[0], w_in_b, tm=N_META, tn=1024)

    y_meta_pad = jnp.pad(y_meta, ((0, META_PAD - N_META), (0, 0)))
    attn, attn_meta, *cast = _attention(lams, diff_subln_w[0], y, y_meta_pad,
                                        later if in_attention else (), t=t_attn)
    attn_meta = attn_meta[:, :N_META]
    w_out_b, w_up_b, w_down_b = cast if in_attention else (w.astype(BF16) for w in later)

    cs, real_rows, meta_rows = _dft_tables(seq)
    four_w = N_GROUPS * GROUP_DIM
    t = _chan_dft(y, cs, tr=min(1024, seq)).reshape(b, 2 * seq, four_w)
    t_meta = _chan_dft(y_meta[None], cs, tr=N_META).reshape(1, 2 * N_META, four_w)
    scale = 1.0 / math.sqrt((N_META + seq) * GROUP_DIM)
    four = _pos_dft(*real_rows, t, t_meta, fourier_norm_w[0], tp=512, scale=scale)
    four_meta = _pos_dft(*meta_rows, t, t_meta, fourier_norm_w[0], tp=N_META, scale=scale)

    attn_w = attn.shape[-1]
    h_mid = _matmul_res([(attn.reshape(m, attn_w), w_out_b, 0), (four.reshape(m, four_w), w_out_b, 1)],
                        x.reshape(m, d), tm=1024, tn=min(1024, d))
    meta_res = jnp.broadcast_to(meta_tokens[None], (b, N_META, d)).reshape(b * N_META, d)
    h_mid_meta = _matmul_res([(attn_meta.reshape(b * N_META, attn_w), w_out_b, 0),
                              (four_meta.reshape(b * N_META, four_w), w_out_b, 1)],
                             meta_res, tm=b * N_META, tn=min(1024, d))

    tm_up = min(1024, seq)
    halo = _gate_halo(h_mid, h_mid_meta, norm2_w[0], w_up_b, b=b, seq=seq, tm=tm_up, d_ff=d_ff)
    z = _up_glu(h_mid, norm2_w[0], w_up_b, halo, conv_w[0], conv_b[0], tm=tm_up, tc=512)
    h_out = _matmul_res([(z, w_down_b, 0)], h_mid, tm=512, tn=min(512, d))

    return _rmsnorm(h_out, final_norm_w, tm=512).reshape(b, seq, d)
```

```python
import functools
import math

import jax
import jax.numpy as jnp
from jax import lax
from jax.experimental import pallas as pl
from jax.experimental.pallas import tpu as pltpu

N_META = 16
N_HEADS = 16
HEAD_DIM = 64
V_DIM = 128
LANES = 128
N_GROUPS = 4
GROUP_DIM = 512
CONV_WIDTH = 3
NORM_EPS = 1e-6
SUBLN_EPS = 1e-5
LAMBDA_INIT = 0.8 - 0.6 * math.exp(-0.3 * 0)

VMEM_LIMIT_BYTES = 56 * 1024 * 1024
BF16 = jnp.bfloat16
F32 = jnp.float32


def _params(*sem):
    return pltpu.CompilerParams(dimension_semantics=sem, vmem_limit_bytes=VMEM_LIMIT_BYTES)


def _norm_rows(x_ref, nw_ref, rows):
    x = x_ref[rows, :]
    inv = lax.rsqrt(jnp.mean(x * x, axis=-1, keepdims=True) + NORM_EPS)
    return (x * inv * nw_ref[...]).astype(BF16)


def _norm_matmul_kernel(x_ref, nw_ref, w_ref, o_ref, hn_ref, *, row_chunks):
    @pl.when(pl.program_id(1) == 0)
    def _():
        for rows in row_chunks:
            hn = _norm_rows(x_ref, nw_ref, rows)
            hn_ref[rows, :] = hn
            o_ref[rows, :] = jnp.dot(hn, w_ref[...], preferred_element_type=F32).astype(o_ref.dtype)

    @pl.when(pl.program_id(1) != 0)
    def _():
        o_ref[...] = jnp.dot(hn_ref[...], w_ref[...], preferred_element_type=F32).astype(o_ref.dtype)


def _norm_matmul(x, nw, w, *, tm, tn, n_cols=None):
    m, d = x.shape
    n = w.shape[1] if n_cols is None else n_cols
    step = min(tm, NORM_ROWS)
    row_chunks = tuple(slice(r, r + step) for r in range(0, tm, step))
    return pl.pallas_call(
        functools.partial(_norm_matmul_kernel, row_chunks=row_chunks),
        grid=(m // tm, n // tn),
        in_specs=[
            pl.BlockSpec((tm, d), lambda i, j: (i, 0)),
            pl.BlockSpec((1, d), lambda i, j: (0, 0)),
            pl.BlockSpec((d, tn), lambda i, j: (0, j)),
        ],
        out_specs=pl.BlockSpec((tm, tn), lambda i, j: (i, j)),
        out_shape=jax.ShapeDtypeStruct((m, n), BF16),
        scratch_shapes=[pltpu.VMEM((tm, d), BF16)],
        compiler_params=_params("parallel", "arbitrary"),
        name="norm_matmul",
    )(x, nw.reshape(1, d), w)


META_PAD = 128
SUM_ROWS = 16
LOG2E = 1.0 / math.log(2.0)
BEFORE, DIAG, AFTER = 0, 1, 2
SKIP_BITS = 160.0
NORM_SLACK = 1.01


def _attn_kernel(*refs, t, n_cast):
    (lq1_ref, lk1_ref, lq2_ref, lk2_ref, subw_ref, q_ref, qm_ref, k_ref, v_ref, km_ref,
     vm_ref) = refs[:N_ATTN_INPUTS]
    cast_in = refs[N_ATTN_INPUTS:N_ATTN_INPUTS + n_cast]
    o_ref, om_ref = refs[N_ATTN_INPUTS + n_cast:N_ATTN_INPUTS + n_cast + 2]
    cast_out = refs[N_ATTN_INPUTS + n_cast + 2:N_ATTN_INPUTS + 2 * n_cast + 2]
    (vt_ref, vtm_ref, bias_ref, ta_ref, tb_ref, ma_ref, mb_ref, m_ref, acc_ref,
     knorm_ref) = refs[N_ATTN_INPUTS + 2 * n_cast + 2:]
    for w_ref, wb_ref in zip(cast_in, cast_out):
        wb_ref[...] = w_ref[...].astype(wb_ref.dtype)

    h = pl.program_id(1)
    qi = pl.program_id(2)
    seq = k_ref.shape[0]
    n_chunks = seq // t
    assert n_chunks >= 2, "every query tile needs a neighbouring key chunk"
    acc_rows = V_DIM + SUM_ROWS

    slope = LOG2E * jnp.exp2(jnp.full((1, 1), -8.0 / N_HEADS, F32) * jnp.asarray(h + 1, F32))
    lane = lax.broadcasted_iota(jnp.int32, (1, V_DIM), 1)
    map_cols = (lax.broadcasted_iota(jnp.int32, (V_DIM, V_DIM), 0) // HEAD_DIM
                == lax.broadcasted_iota(jnp.int32, (V_DIM, V_DIM), 1)).astype(BF16)
    map_rows = (lax.broadcasted_iota(jnp.int32, (8, V_DIM), 1) // HEAD_DIM
                == lax.broadcasted_iota(jnp.int32, (8, V_DIM), 0)).astype(BF16)

    @pl.when(qi == 0)
    def _():
        for c in range(n_chunks):
            vt_ref[0:V_DIM, c * t:(c + 1) * t] = v_ref[c * t:(c + 1) * t, :].astype(F32).T.astype(BF16)
        vt_ref[V_DIM:acc_rows, :] = jnp.ones((SUM_ROWS, seq), BF16)
        vtm_ref[0:V_DIM, :] = vm_ref[...].astype(F32).T.astype(BF16)
        vtm_ref[V_DIM:acc_rows, :] = jnp.ones((SUM_ROWS, META_PAD), BF16)
        row = lax.broadcasted_iota(jnp.int32, (t, t), 0)
        col = lax.broadcasted_iota(jnp.int32, (t, t), 1)
        bias_ref[BEFORE] = slope * row.astype(F32)
        bias_ref[DIAG] = -slope * jnp.abs(col - row).astype(F32)
        bias_ref[AFTER] = -slope * row.astype(F32)
        k_sq = jnp.square(k_ref[...].astype(F32)).astype(BF16)
        sq_norms = jnp.dot(k_sq, map_cols, preferred_element_type=F32)
        knorm_ref[0:1, :] = jnp.sqrt(jnp.max(sq_norms, axis=0, keepdims=True))

    lam = (jnp.exp(jnp.sum(lq1_ref[...] * lk1_ref[...], axis=-1, keepdims=True))
           - jnp.exp(jnp.sum(lq2_ref[...] * lk2_ref[...], axis=-1, keepdims=True))
           + LAMBDA_INIT)
    meta_row = lax.broadcasted_iota(jnp.int32, (N_META, 1), 0)

    def attend(q, q_pos, tile_idx, out_ref):
        nq = q.shape[0]
        q32 = q.astype(F32) * (LOG2E / math.sqrt(HEAD_DIM))
        qs = (jnp.where(lane < HEAD_DIM, q32, 0.0).astype(BF16),
              jnp.where(lane >= HEAD_DIM, q32, 0.0).astype(BF16))

        def scores(k_c):
            return tuple(lax.dot_general(k_c, qc, (((1,), (1,)), ((), ())), preferred_element_type=F32)
                         for qc in qs)

        def update(c, tt, tt_max, row_bias, vt_c):
            m = m_ref[c, 0:1, 0:nq]
            m_new = jnp.maximum(m, tt_max + row_bias)
            p = jnp.exp2(tt - (m_new - row_bias))
            acc_ref[c, :, 0:nq] = (jnp.exp2(m - m_new) * acc_ref[c, :, 0:nq]
                                   + jnp.dot(vt_c, p.astype(BF16), preferred_element_type=F32))
            m_ref[c, 0:1, 0:nq] = m_new

        def produce(c_idx, kind, buf_ref, max_ref):
            k_c = k_ref[pl.ds(pl.multiple_of(c_idx * t, t), t), :]
            key_bias = bias_ref[kind, :, 0:nq]
            for c, s in enumerate(scores(k_c)):
                tt = s + key_bias
                buf_ref[c, :, 0:nq] = tt
                max_ref[c, 0:1, 0:nq] = jnp.max(tt, axis=0, keepdims=True)

        def consume(c_idx, kind, buf_ref, max_ref):
            sign = 1.0 - kind
            k_pos0 = jnp.asarray(N_META + c_idx * t, F32)
            row_bias = (-sign) * slope * (q_pos - k_pos0)
            vt_c = vt_ref[:, pl.ds(pl.multiple_of(c_idx * t, t), t)]
            for c in range(2):
                update(c, buf_ref[c, :, 0:nq], max_ref[c, 0:1, 0:nq], row_bias, vt_c)

        m_ref[...] = jnp.full(m_ref.shape, -jnp.inf, F32)
        acc_ref[...] = jnp.zeros(acc_ref.shape, F32)
        buf_a, buf_b = (ta_ref, ma_ref), (tb_ref, mb_ref)
        if tile_idx is None:
            first = (jnp.int32(0), AFTER)
        else:
            produce(tile_idx, DIAG, *buf_a)
            has_before = tile_idx > 0
            first = (jnp.where(has_before, tile_idx - 1, tile_idx + 1),
                     jnp.where(has_before, BEFORE, AFTER))

        if tile_idx is None:
            key_bias, row_bias = -slope * jnp.abs(q_pos - meta_row.astype(F32)), jnp.zeros((1, nq), F32)
        else:
            key_bias, row_bias = bias_ref[BEFORE, 0:N_META, 0:nq], -slope * q_pos
        for c, s in enumerate(scores(km_ref[0:N_META, :])):
            tt = s + key_bias
            update(c, tt, jnp.max(tt, axis=0, keepdims=True), row_bias, vtm_ref[:, 0:N_META])

        produce(*first, *buf_b)
        if tile_idx is not None:
            consume(tile_idx, DIAG, *buf_a)

        q_sq = jnp.square(q32.astype(BF16).astype(F32)).astype(BF16)
        q_norm = jnp.sqrt(lax.dot_general(map_rows, q_sq, (((1,), (1,)), ((), ())),
                                          preferred_element_type=F32))
        bound = jnp.full((1, 1), -jnp.inf, F32)
        for c in range(2):
            gap = knorm_ref[0:1, c:c + 1] * q_norm[c:c + 1] * NORM_SLACK - m_ref[c, 0:1, 0:nq]
            bound = jnp.maximum(bound, jnp.max(gap, axis=-1, keepdims=True))
        reach = ((bound + SKIP_BITS) / slope - 1.0) / t + 1.0
        n_side = jnp.max(jnp.clip(jnp.floor(reach), 1.0, float(n_chunks))).astype(jnp.int32)
        if tile_idx is None:
            n_before, n_todo = 0, jnp.minimum(n_chunks, n_side)
        else:
            n_before = jnp.minimum(tile_idx, n_side)
            n_todo = n_before + jnp.minimum(n_chunks - 1 - tile_idx, n_side)

        def todo(k):
            if tile_idx is None:
                return k, AFTER
            before = k < n_before
            c_idx = jnp.where(before, tile_idx - 1 - k, tile_idx + 1 + k - n_before)
            return c_idx, jnp.where(before, BEFORE, AFTER)

        def pair(j, carry):
            k = 2 * j
            produce(*todo(k + 1), *buf_a)
            consume(*todo(k), *buf_b)
            produce(*todo(k + 2), *buf_b)
            consume(*todo(k + 1), *buf_a)
            return carry

        n_pairs = (n_todo - 1) // 2
        lax.fori_loop(0, n_pairs, pair, 0)

        @pl.when(n_todo - 2 * n_pairs == 2)
        def _():
            produce(*todo(n_todo - 1), *buf_a)
            consume(*todo(n_todo - 2), *buf_b)
            consume(*todo(n_todo - 1), *buf_a)

        @pl.when(n_todo - 2 * n_pairs == 1)
        def _():
            consume(*todo(n_todo - 1), *buf_b)

        acc1, acc2 = acc_ref[0, :, 0:nq], acc_ref[1, :, 0:nq]
        a = (acc1[0:V_DIM] * (1.0 / acc1[V_DIM:V_DIM + 1])
             - acc2[0:V_DIM] * (lam / acc2[V_DIM:V_DIM + 1]))
        inv = lax.rsqrt(jnp.mean(a * a, axis=0, keepdims=True) + SUBLN_EPS)
        out_ref[...] = ((a * inv).T * subw_ref[...] * (1.0 - LAMBDA_INIT)).astype(out_ref.dtype)

    q_pos = (N_META + qi * t + lax.broadcasted_iota(jnp.int32, (1, t), 1)).astype(F32)
    attend(q_ref[...], q_pos, qi, o_ref)

    @pl.when(qi == 0)
    def _():
        pos = jnp.minimum(lax.broadcasted_iota(jnp.int32, (1, META_PAD), 1), N_META - 1)
        attend(qm_ref[...], pos.astype(F32), None, om_ref)


N_ATTN_INPUTS = 11
BF16_ROWS = 16


def _slab_rows(w, n_steps):
    n_rows = w.shape[0]
    for rows in range(BF16_ROWS, n_rows + 1, BF16_ROWS):
        if n_rows % rows == 0 and n_rows // rows <= n_steps:
            return rows
    return None


def _attention(lams, subw, y, y_meta_pad, cast_weights, *, t):
    b, seq, _ = y.shape
    n_q = seq // t
    h_blk = N_HEADS
    vec = lambda n: pl.BlockSpec((1, n), lambda bi, h, i: (0, 0))

    def slab(w):
        rows = _slab_rows(w, b * N_HEADS * n_q)
        last = w.shape[0] // rows - 1
        return pl.BlockSpec((rows, w.shape[1]),
                            lambda bi, h, i: (jnp.minimum((bi * N_HEADS + h) * n_q + i, last), 0))

    acc_rows = V_DIM + SUM_ROWS
    return pl.pallas_call(
        functools.partial(_attn_kernel, t=t, n_cast=len(cast_weights)),
        grid=(b, N_HEADS, n_q),
        in_specs=[
            vec(HEAD_DIM), vec(HEAD_DIM), vec(HEAD_DIM), vec(HEAD_DIM), vec(V_DIM),
            pl.BlockSpec((None, t, V_DIM), lambda bi, h, i: (bi, i, h)),
            pl.BlockSpec((META_PAD, V_DIM), lambda bi, h, i: (0, h)),
            pl.BlockSpec((None, seq, V_DIM), lambda bi, h, i: (bi, 0, h_blk + h)),
            pl.BlockSpec((None, seq, V_DIM), lambda bi, h, i: (bi, 0, 2 * h_blk + h)),
            pl.BlockSpec((META_PAD, V_DIM), lambda bi, h, i: (0, h_blk + h)),
            pl.BlockSpec((META_PAD, V_DIM), lambda bi, h, i: (0, 2 * h_blk + h)),
        ] + [slab(w) for w in cast_weights],
        out_specs=[
            pl.BlockSpec((None, t, V_DIM), lambda bi, h, i: (bi, i, h)),
            pl.BlockSpec((None, META_PAD, V_DIM), lambda bi, h, i: (bi, 0, h)),
        ] + [slab(w) for w in cast_weights],
        out_shape=[
            jax.ShapeDtypeStruct((b, seq, N_HEADS * V_DIM), BF16),
            jax.ShapeDtypeStruct((b, META_PAD, N_HEADS * V_DIM), BF16),
        ] + [jax.ShapeDtypeStruct(w.shape, BF16) for w in cast_weights],
        scratch_shapes=[
            pltpu.VMEM((acc_rows, seq), BF16),
            pltpu.VMEM((acc_rows, META_PAD), BF16),
            pltpu.VMEM((3, t, t), F32),
            pltpu.VMEM((2, t, t), F32),
            pltpu.VMEM((2, t, t), F32),
            pltpu.VMEM((2, 8, t), F32),
            pltpu.VMEM((2, 8, t), F32),
            pltpu.VMEM((2, 8, t), F32),
            pltpu.VMEM((2, acc_rows, t), F32),
            pltpu.VMEM((8, V_DIM), F32),
        ],
        compiler_params=_params("arbitrary", "arbitrary", "arbitrary"),
        name="diff_attention",
    )(*[v.reshape(1, HEAD_DIM) for v in lams], subw.reshape(1, V_DIM),
      y, y_meta_pad, y, y, y_meta_pad, y_meta_pad, *cast_weights)


def _chan_dft_kernel(f_ref, cs_ref, o_ref):
    t = jnp.dot(f_ref[...], cs_ref[...], preferred_element_type=F32)
    o_ref[0] = t[:, :GROUP_DIM].astype(o_ref.dtype)
    o_ref[1] = t[:, GROUP_DIM:].astype(o_ref.dtype)


def _chan_dft(y, cs, *, tr):
    b, rows, in_width = y.shape
    f_blk0 = in_width // GROUP_DIM - N_GROUPS
    return pl.pallas_call(
        _chan_dft_kernel,
        grid=(b, rows // tr, N_GROUPS),
        in_specs=[
            pl.BlockSpec((None, tr, GROUP_DIM), lambda bi, r, g: (bi, r, f_blk0 + g)),
            pl.BlockSpec((GROUP_DIM, 2 * GROUP_DIM), lambda bi, r, g: (0, 0)),
        ],
        out_specs=pl.BlockSpec((None, 2, tr, GROUP_DIM), lambda bi, r, g: (bi, 0, r, g)),
        out_shape=jax.ShapeDtypeStruct((b, 2, rows, N_GROUPS * GROUP_DIM), BF16),
        compiler_params=_params("parallel", "parallel", "parallel"),
        name="chan_dft",
    )(y, cs)


def _pos_dft_kernel(mr_ref, mm_ref, t_ref, tm_ref, nw_ref, o_ref, *, scale):
    r = jnp.dot(mr_ref[...], t_ref[...], preferred_element_type=F32)
    r = r + jnp.dot(mm_ref[...], tm_ref[...], preferred_element_type=F32)
    r = r * scale
    inv = lax.rsqrt(jnp.mean(r * r, axis=-1, keepdims=True) + NORM_EPS)
    o_ref[...] = (r * inv * nw_ref[...]).astype(o_ref.dtype)


def _pos_dft(mr, mm, t, t_meta, nw, *, tp, scale):
    p_rows = mr.shape[0]
    b, k_real, width = t.shape
    k_meta = t_meta.shape[1]
    return pl.pallas_call(
        functools.partial(_pos_dft_kernel, scale=scale),
        grid=(p_rows // tp, b, N_GROUPS),
        in_specs=[
            pl.BlockSpec((tp, k_real), lambda p, bi, g: (p, 0)),
            pl.BlockSpec((tp, k_meta), lambda p, bi, g: (p, 0)),
            pl.BlockSpec((None, k_real, GROUP_DIM), lambda p, bi, g: (bi, 0, g)),
            pl.BlockSpec((None, k_meta, GROUP_DIM), lambda p, bi, g: (0, 0, g)),
            pl.BlockSpec((1, GROUP_DIM), lambda p, bi, g: (0, g)),
        ],
        out_specs=pl.BlockSpec((None, tp, GROUP_DIM), lambda p, bi, g: (bi, p, g)),
        out_shape=jax.ShapeDtypeStruct((b, p_rows, width), BF16),
        compiler_params=_params("parallel", "parallel", "arbitrary"),
        name="pos_dft",
    )(mr, mm, t, t_meta, nw.reshape(1, width))


TABLE_STEP = 64


def _dft_tables(seq):
    l_total = N_META + seq
    c = jnp.arange(GROUP_DIM, dtype=jnp.int32)
    ang_c = ((c[:, None] * c[None, :]) % GROUP_DIM).astype(F32) * (2.0 * math.pi / GROUP_DIM)
    cs = jnp.concatenate([jnp.cos(ang_c), jnp.sin(ang_c)], axis=1).astype(BF16)

    to_rad = 2.0 * math.pi / l_total
    half_pi = jnp.asarray([0.0, 0.5 * math.pi], F32)
    meta_p = jnp.arange(N_META, dtype=jnp.int32)
    real_p = N_META + jnp.arange(seq, dtype=jnp.int32)
    real_l, real_phase = jnp.tile(real_p, 2), jnp.repeat(half_pi, seq)
    meta_l, meta_phase = jnp.tile(meta_p, 2), jnp.repeat(half_pi, N_META)

    def angle(p, l, phase):
        return ((p[:, None] * l[None, :]) % l_total).astype(F32) * to_rad + phase[None, :]

    def direct(p, l, phase):
        return jnp.cos(angle(p, l, phase)).astype(BF16)

    hi = angle(N_META + TABLE_STEP * jnp.arange(seq // TABLE_STEP, dtype=jnp.int32), real_l, 0.0 * real_phase)
    lo = angle(jnp.arange(TABLE_STEP, dtype=jnp.int32), real_l, real_phase)
    real_rows_real_cols = (jnp.cos(hi)[:, None, :] * jnp.cos(lo)[None, :, :]
                           - jnp.sin(hi)[:, None, :] * jnp.sin(lo)[None, :, :]
                           ).reshape(seq, 2 * seq).astype(BF16)
    real_rows = (real_rows_real_cols, direct(real_p, meta_l, meta_phase))
    meta_rows = (direct(meta_p, real_l, real_phase), direct(meta_p, meta_l, meta_phase))
    return cs, real_rows, meta_rows


def _matmul_res_kernel(*refs, n_pairs):
    res_ref, o_ref = refs[2 * n_pairs], refs[2 * n_pairs + 1]
    acc = res_ref[...]
    for k in range(n_pairs):
        acc = acc + jnp.dot(refs[2 * k][...], refs[2 * k + 1][...], preferred_element_type=F32)
    o_ref[...] = acc


def _matmul_res(pairs, res, *, tm, tn):
    m, n = res.shape
    in_specs, args = [], []
    for a, w, blk in pairs:
        k = a.shape[1]
        in_specs += [pl.BlockSpec((tm, k), lambda i, j: (i, 0)),
                     pl.BlockSpec((k, tn), lambda i, j, blk=blk: (blk, j))]
        args += [a, w]
    in_specs.append(pl.BlockSpec((tm, tn), lambda i, j: (i, j)))
    return pl.pallas_call(
        functools.partial(_matmul_res_kernel, n_pairs=len(pairs)),
        grid=(m // tm, n // tn),
        in_specs=in_specs,
        out_specs=pl.BlockSpec((tm, tn), lambda i, j: (i, j)),
        out_shape=jax.ShapeDtypeStruct((m, n), F32),
        compiler_params=_params("parallel", "arbitrary"),
        name="matmul_res",
    )(*args, res)


HALO_ROWS = 8
NORM_ROWS = 256


def _up_glu_kernel(x_ref, nw_ref, wg_ref, wv_ref, halo_ref, cw_ref, cb_ref, o_ref, hn_ref, *, tm):
    @pl.when(pl.program_id(1) == 0)
    def _():
        def norm_rows(r, carry):
            rows = pl.ds(pl.multiple_of(r * NORM_ROWS, NORM_ROWS), NORM_ROWS)
            hn_ref[rows, :] = _norm_rows(x_ref, nw_ref, rows)
            return carry
        lax.fori_loop(0, tm // NORM_ROWS, norm_rows, 0)

    hn = hn_ref[...]
    g = jnp.dot(hn, wg_ref[...], preferred_element_type=F32)
    v = jnp.dot(hn, wv_ref[...], preferred_element_type=F32)
    row = lax.broadcasted_iota(jnp.int32, g.shape, 0)
    g_prev = jnp.where(row == 0, halo_ref[0, 0:1, :], pltpu.roll(g, 1, axis=0))
    g_next = jnp.where(row == tm - 1, halo_ref[0, 1:2, :], pltpu.roll(g, tm - 1, axis=0))
    conv = g_prev * cw_ref[0:1, :] + g * cw_ref[1:2, :] + g_next * cw_ref[2:3, :] + cb_ref[...]
    gelu = 0.5 * conv * (1.0 + lax.erf(conv * math.sqrt(0.5)))
    o_ref[...] = (gelu * v).astype(o_ref.dtype)


def _up_glu(x, nw, w_up, halo, conv_w, conv_b, *, tm, tc):
    m, d = x.shape
    d_ff = w_up.shape[1] // 2
    assert d_ff % LANES == 0 and d_ff >= tc
    n_c = pl.cdiv(d_ff, tc)
    el = pl.Element

    def col(c, base=0):
        return pl.multiple_of(base + jnp.minimum(c * tc, d_ff - tc), LANES)

    return pl.pallas_call(
        functools.partial(_up_glu_kernel, tm=tm),
        grid=(m // tm, n_c),
        in_specs=[
            pl.BlockSpec((tm, d), lambda i, c: (i, 0), pipeline_mode=pl.Buffered(1)),
            pl.BlockSpec((1, d), lambda i, c: (0, 0)),
            pl.BlockSpec((el(d), el(tc)), lambda i, c: (0, col(c))),
            pl.BlockSpec((el(d), el(tc)), lambda i, c: (0, col(c, d_ff))),
            pl.BlockSpec((el(1), el(HALO_ROWS), el(tc)), lambda i, c: (i, 0, col(c))),
            pl.BlockSpec((el(CONV_WIDTH), el(tc)), lambda i, c: (0, col(c))),
            pl.BlockSpec((el(1), el(tc)), lambda i, c: (0, col(c))),
        ],
        out_specs=pl.BlockSpec((el(tm), el(tc)), lambda i, c: (pl.multiple_of(i * tm, tm), col(c))),
        out_shape=jax.ShapeDtypeStruct((m, d_ff), BF16),
        scratch_shapes=[pltpu.VMEM((tm, d), BF16)],
        compiler_params=_params("parallel", "arbitrary"),
        name="up_glu",
    )(x, nw.reshape(1, d), w_up, w_up, halo, conv_w, conv_b.reshape(1, d_ff))


def _gate_halo(h_mid, h_mid_meta, nw, w_up, *, b, seq, tm, d_ff):
    d = h_mid.shape[-1]
    n_t = seq // tm
    h3 = h_mid.reshape(b, seq, d)
    pairs = [h3[:, k * tm - 1:k * tm + 1] for k in range(1, n_t)]
    meta_last = h_mid_meta.reshape(b, N_META, d)[:, N_META - 1:]
    rows = jnp.concatenate([meta_last] + pairs, axis=1).reshape(-1, d)
    n_rows = rows.shape[0]
    pad = -n_rows % 16
    rows = jnp.pad(rows, ((0, pad), (0, 0)))
    gates = _norm_matmul(rows, nw, w_up, tm=n_rows + pad, tn=256, n_cols=d_ff)[:n_rows]
    gates = gates.astype(F32).reshape(b, 2 * n_t - 1, d_ff)
    before = jnp.concatenate([gates[:, 0:1], gates[:, 1::2]], axis=1)
    after = jnp.concatenate([gates[:, 2::2], jnp.zeros((b, 1, d_ff), F32)], axis=1)
    halo = jnp.stack([before, after], axis=2)
    halo = jnp.pad(halo, ((0, 0), (0, 0), (0, HALO_ROWS - 2), (0, 0)))
    return halo.reshape(b * n_t, HALO_ROWS, d_ff)


def _rmsnorm_kernel(x_ref, w_ref, o_ref):
    x = x_ref[...]
    inv = lax.rsqrt(jnp.mean(x * x, axis=-1, keepdims=True) + NORM_EPS)
    o_ref[...] = x * inv * w_ref[...]


def _rmsnorm(x, w, *, tm):
    m, d = x.shape
    return pl.pallas_call(
        _rmsnorm_kernel,
        grid=(m // tm,),
        in_specs=[pl.BlockSpec((tm, d), lambda i: (i, 0)), pl.BlockSpec((1, d), lambda i: (0, 0))],
        out_specs=pl.BlockSpec((tm, d), lambda i: (i, 0)),
        out_shape=jax.ShapeDtypeStruct((m, d), F32),
        compiler_params=_params("parallel"),
        name="final_rmsnorm",
    )(x, w.reshape(1, d))


def kernel(x, meta_tokens, norm1_w, w_in, lambda_q1, lambda_k1, lambda_q2, lambda_k2,
           diff_subln_w, fourier_norm_w, w_out, norm2_w, w_up, conv_w, conv_b, w_down,
           final_norm_w):
    b, seq, d = x.shape
    m = b * seq
    in_width = w_in.shape[-1]
    d_ff = w_down.shape[1]
    assert w_in.shape[0] == 1 and meta_tokens.shape[0] == N_META

    w_in_b = w_in[0].astype(BF16)
    lams = (lambda_q1[0], lambda_k1[0], lambda_q2[0], lambda_k2[0])
    t_attn = min(512, seq)

    later = (w_out[0], w_up[0], w_down[0])
    in_attention = all(_slab_rows(w, b * N_HEADS * (seq // t_attn)) for w in later)

    y = _norm_matmul(x.reshape(m, d), norm1_w[0], w_in_b, tm=512, tn=1024).reshape(b, seq, in_width)
    y_meta = _norm_matmul(meta_tokens, norm1_w[0], w_in_b, tm=N_META, tn=1024)

    y_meta_pad = jnp.pad(y_meta, ((0, META_PAD - N_META), (0, 0)))
    attn, attn_meta, *cast = _attention(lams, diff_subln_w[0], y, y_meta_pad,
                                        later if in_attention else (), t=t_attn)
    attn_meta = attn_meta[:, :N_META]
    w_out_b, w_up_b, w_down_b = cast if in_attention else (w.astype(BF16) for w in later)

    cs, real_rows, meta_rows = _dft_tables(seq)
    four_w = N_GROUPS * GROUP_DIM
    t = _chan_dft(y, cs, tr=min(1024, seq)).reshape(b, 2 * seq, four_w)
    t_meta = _chan_dft(y_meta[None], cs, tr=N_META).reshape(1, 2 * N_META, four_w)
    scale = 1.0 / math.sqrt((N_META + seq) * GROUP_DIM)
    four = _pos_dft(*real_rows, t, t_meta, fourier_norm_w[0], tp=min(1024, seq), scale=scale)
    four_meta = _pos_dft(*meta_rows, t, t_meta, fourier_norm_w[0], tp=N_META, scale=scale)

    attn_w = attn.shape[-1]
    h_mid = _matmul_res([(attn.reshape(m, attn_w), w_out_b, 0), (four.reshape(m, four_w), w_out_b, 1)],
                        x.reshape(m, d), tm=1024, tn=min(1024, d))
    meta_res = jnp.broadcast_to(meta_tokens[None], (b, N_META, d)).reshape(b * N_META, d)
    h_mid_meta = _matmul_res([(attn_meta.reshape(b * N_META, attn_w), w_out_b, 0),
                              (four_meta.reshape(b * N_META, four_w), w_out_b, 1)],
                             meta_res, tm=b * N_META, tn=min(1024, d))

    tm_up = min(1024, seq)
    halo = _gate_halo(h_mid, h_mid_meta, norm2_w[0], w_up_b, b=b, seq=seq, tm=tm_up, d_ff=d_ff)
    z = _up_glu(h_mid, norm2_w[0], w_up_b, halo, conv_w[0], conv_b[0], tm=tm_up, tc=512)
    h_out = _matmul_res([(z, w_down_b, 0)], h_mid, tm=512, tn=min(512, d))

    return _rmsnorm(h_out, final_norm_w, tm=512).reshape(b, seq, d)
```

```python
import functools
import math
from typing import NamedTuple

import jax
import jax.numpy as jnp
from jax import lax
from jax.experimental import pallas as pl
from jax.experimental.pallas import tpu as pltpu

N_META = 16
N_HEADS = 16
HEAD_DIM = 64
V_DIM = 128
LANES = 128
N_GROUPS = 4
GROUP_DIM = 512
CONV_WIDTH = 3
NORM_EPS = 1e-6
SUBLN_EPS = 1e-5
LAMBDA_INIT = 0.8 - 0.6 * math.exp(-0.3 * 0)

VMEM_LIMIT_BYTES = 56 * 1024 * 1024
NORM_ROWS = 256
BF16 = jnp.bfloat16
F32 = jnp.float32


def _params(*sem):
    return pltpu.CompilerParams(dimension_semantics=sem, vmem_limit_bytes=VMEM_LIMIT_BYTES)


def _norm_rows(x_ref, nw_ref, rows):
    x = x_ref[rows, :]
    inv = lax.rsqrt(jnp.mean(x * x, axis=-1, keepdims=True) + NORM_EPS)
    return (x * inv * nw_ref[...]).astype(BF16)


def _norm_matmul_kernel(x_ref, nw_ref, w_ref, o_ref, hn_ref, *, row_chunks):
    @pl.when(pl.program_id(1) == 0)
    def _():
        for rows in row_chunks:
            hn = _norm_rows(x_ref, nw_ref, rows)
            hn_ref[rows, :] = hn
            o_ref[rows, :] = jnp.dot(hn, w_ref[...], preferred_element_type=F32).astype(o_ref.dtype)

    @pl.when(pl.program_id(1) != 0)
    def _():
        o_ref[...] = jnp.dot(hn_ref[...], w_ref[...], preferred_element_type=F32).astype(o_ref.dtype)


def _norm_matmul(x, nw, w, *, tm, tn, n_cols=None):
    m, d = x.shape
    n = w.shape[1] if n_cols is None else n_cols
    step = min(tm, NORM_ROWS)
    row_chunks = tuple(slice(r, r + step) for r in range(0, tm, step))
    return pl.pallas_call(
        functools.partial(_norm_matmul_kernel, row_chunks=row_chunks),
        grid=(m // tm, n // tn),
        in_specs=[
            pl.BlockSpec((tm, d), lambda i, j: (i, 0)),
            pl.BlockSpec((1, d), lambda i, j: (0, 0)),
            pl.BlockSpec((d, tn), lambda i, j: (0, j)),
        ],
        out_specs=pl.BlockSpec((tm, tn), lambda i, j: (i, j)),
        out_shape=jax.ShapeDtypeStruct((m, n), BF16),
        scratch_shapes=[pltpu.VMEM((tm, d), BF16)],
        compiler_params=_params("parallel", "arbitrary"),
        name="norm_matmul",
    )(x, nw.reshape(1, d), w)


META_PAD = 128
SUM_ROWS = 16
LOG2E = 1.0 / math.log(2.0)
BEFORE, DIAG, AFTER = 0, 1, 2
SKIP_BITS = 160.0
NORM_SLACK = 1.01


def _attn_kernel(*refs, t, n_cast):
    (lq1_ref, lk1_ref, lq2_ref, lk2_ref, subw_ref, q_ref, qm_ref, k_ref, v_ref, km_ref,
     vm_ref) = refs[:N_ATTN_INPUTS]
    cast_in = refs[N_ATTN_INPUTS:N_ATTN_INPUTS + n_cast]
    o_ref, om_ref = refs[N_ATTN_INPUTS + n_cast:N_ATTN_INPUTS + n_cast + 2]
    cast_out = refs[N_ATTN_INPUTS + n_cast + 2:N_ATTN_INPUTS + 2 * n_cast + 2]
    (vt_ref, vtm_ref, bias_ref, ta_ref, tb_ref, ma_ref, mb_ref, m_ref, acc_ref,
     knorm_ref) = refs[N_ATTN_INPUTS + 2 * n_cast + 2:]
    for w_ref, wb_ref in zip(cast_in, cast_out):
        wb_ref[...] = w_ref[...].astype(wb_ref.dtype)

    h = pl.program_id(1)
    qi = pl.program_id(2)
    seq = k_ref.shape[0]
    n_chunks = seq // t
    assert n_chunks >= 2, "every query tile needs a neighbouring key chunk"
    acc_rows = V_DIM + SUM_ROWS

    slope = LOG2E * jnp.exp2(jnp.full((1, 1), -8.0 / N_HEADS, F32) * jnp.asarray(h + 1, F32))
    lane = lax.broadcasted_iota(jnp.int32, (1, V_DIM), 1)
    map_cols = (lax.broadcasted_iota(jnp.int32, (V_DIM, V_DIM), 0) // HEAD_DIM
                == lax.broadcasted_iota(jnp.int32, (V_DIM, V_DIM), 1)).astype(BF16)
    map_rows = (lax.broadcasted_iota(jnp.int32, (8, V_DIM), 1) // HEAD_DIM
                == lax.broadcasted_iota(jnp.int32, (8, V_DIM), 0)).astype(BF16)

    @pl.when(qi == 0)
    def _():
        for c in range(n_chunks):
            vt_ref[0:V_DIM, c * t:(c + 1) * t] = v_ref[c * t:(c + 1) * t, :].astype(F32).T.astype(BF16)
        vt_ref[V_DIM:acc_rows, :] = jnp.ones((SUM_ROWS, seq), BF16)
        vtm_ref[0:V_DIM, :] = vm_ref[...].astype(F32).T.astype(BF16)
        vtm_ref[V_DIM:acc_rows, :] = jnp.ones((SUM_ROWS, META_PAD), BF16)
        row = lax.broadcasted_iota(jnp.int32, (t, t), 0)
        col = lax.broadcasted_iota(jnp.int32, (t, t), 1)
        bias_ref[BEFORE] = slope * row.astype(F32)
        bias_ref[DIAG] = -slope * jnp.abs(col - row).astype(F32)
        bias_ref[AFTER] = -slope * row.astype(F32)
        k_sq = jnp.square(k_ref[...].astype(F32)).astype(BF16)
        sq_norms = jnp.dot(k_sq, map_cols, preferred_element_type=F32)
        knorm_ref[0:1, :] = jnp.sqrt(jnp.max(sq_norms, axis=0, keepdims=True))

    lam = (jnp.exp(jnp.sum(lq1_ref[...] * lk1_ref[...], axis=-1, keepdims=True))
           - jnp.exp(jnp.sum(lq2_ref[...] * lk2_ref[...], axis=-1, keepdims=True))
           + LAMBDA_INIT)
    meta_row = lax.broadcasted_iota(jnp.int32, (N_META, 1), 0)

    def attend(q, q_pos, tile_idx, out_ref):
        nq = q.shape[0]
        q32 = q.astype(F32) * (LOG2E / math.sqrt(HEAD_DIM))
        qs = (jnp.where(lane < HEAD_DIM, q32, 0.0).astype(BF16),
              jnp.where(lane >= HEAD_DIM, q32, 0.0).astype(BF16))

        def scores(k_c):
            return tuple(lax.dot_general(k_c, qc, (((1,), (1,)), ((), ())), preferred_element_type=F32)
                         for qc in qs)

        def update(c, tt, tt_max, row_bias, vt_c):
            m = m_ref[c, 0:1, 0:nq]
            m_new = jnp.maximum(m, tt_max + row_bias)
            p = jnp.exp2(tt - (m_new - row_bias))
            acc_ref[c, :, 0:nq] = (jnp.exp2(m - m_new) * acc_ref[c, :, 0:nq]
                                   + jnp.dot(vt_c, p.astype(BF16), preferred_element_type=F32))
            m_ref[c, 0:1, 0:nq] = m_new

        def produce(c_idx, kind, buf_ref, max_ref):
            k_c = k_ref[pl.ds(pl.multiple_of(c_idx * t, t), t), :]
            key_bias = bias_ref[kind, :, 0:nq]
            for c, s in enumerate(scores(k_c)):
                tt = s + key_bias
                buf_ref[c, :, 0:nq] = tt
                max_ref[c, 0:1, 0:nq] = jnp.max(tt, axis=0, keepdims=True)

        def consume(c_idx, kind, buf_ref, max_ref):
            sign = 1.0 - kind
            k_pos0 = jnp.asarray(N_META + c_idx * t, F32)
            row_bias = (-sign) * slope * (q_pos - k_pos0)
            vt_c = vt_ref[:, pl.ds(pl.multiple_of(c_idx * t, t), t)]
            for c in range(2):
                update(c, buf_ref[c, :, 0:nq], max_ref[c, 0:1, 0:nq], row_bias, vt_c)

        m_ref[...] = jnp.full(m_ref.shape, -jnp.inf, F32)
        acc_ref[...] = jnp.zeros(acc_ref.shape, F32)
        buf_a, buf_b = (ta_ref, ma_ref), (tb_ref, mb_ref)
        if tile_idx is None:
            first = (jnp.int32(0), AFTER)
        else:
            produce(tile_idx, DIAG, *buf_a)
            has_before = tile_idx > 0
            first = (jnp.where(has_before, tile_idx - 1, tile_idx + 1),
                     jnp.where(has_before, BEFORE, AFTER))

        if tile_idx is None:
            key_bias, row_bias = -slope * jnp.abs(q_pos - meta_row.astype(F32)), jnp.zeros((1, nq), F32)
        else:
            key_bias, row_bias = bias_ref[BEFORE, 0:N_META, 0:nq], -slope * q_pos
        for c, s in enumerate(scores(km_ref[0:N_META, :])):
            tt = s + key_bias
            update(c, tt, jnp.max(tt, axis=0, keepdims=True), row_bias, vtm_ref[:, 0:N_META])

        produce(*first, *buf_b)
        if tile_idx is not None:
            consume(tile_idx, DIAG, *buf_a)

        q_sq = jnp.square(q32.astype(BF16).astype(F32)).astype(BF16)
        q_norm = jnp.sqrt(lax.dot_general(map_rows, q_sq, (((1,), (1,)), ((), ())),
                                          preferred_element_type=F32))
        bound = jnp.full((1, 1), -jnp.inf, F32)
        for c in range(2):
            gap = knorm_ref[0:1, c:c + 1] * q_norm[c:c + 1] * NORM_SLACK - m_ref[c, 0:1, 0:nq]
            bound = jnp.maximum(bound, jnp.max(gap, axis=-1, keepdims=True))
        reach = ((bound + SKIP_BITS) / slope - 1.0) / t + 1.0
        n_side = jnp.max(jnp.clip(jnp.floor(reach), 1.0, float(n_chunks))).astype(jnp.int32)
        if tile_idx is None:
            n_before, n_todo = 0, jnp.minimum(n_chunks, n_side)
        else:
            n_before = jnp.minimum(tile_idx, n_side)
            n_todo = n_before + jnp.minimum(n_chunks - 1 - tile_idx, n_side)

        def todo(k):
            if tile_idx is None:
                return k, AFTER
            before = k < n_before
            c_idx = jnp.where(before, tile_idx - 1 - k, tile_idx + 1 + k - n_before)
            return c_idx, jnp.where(before, BEFORE, AFTER)

        def pair(j, carry):
            k = 2 * j
            produce(*todo(k + 1), *buf_a)
            consume(*todo(k), *buf_b)
            produce(*todo(k + 2), *buf_b)
            consume(*todo(k + 1), *buf_a)
            return carry

        n_pairs = (n_todo - 1) // 2
        lax.fori_loop(0, n_pairs, pair, 0)

        @pl.when(n_todo - 2 * n_pairs == 2)
        def _():
            produce(*todo(n_todo - 1), *buf_a)
            consume(*todo(n_todo - 2), *buf_b)
            consume(*todo(n_todo - 1), *buf_a)

        @pl.when(n_todo - 2 * n_pairs == 1)
        def _():
            consume(*todo(n_todo - 1), *buf_b)

        acc1, acc2 = acc_ref[0, :, 0:nq], acc_ref[1, :, 0:nq]
        a = (acc1[0:V_DIM] * (1.0 / acc1[V_DIM:V_DIM + 1])
             - acc2[0:V_DIM] * (lam / acc2[V_DIM:V_DIM + 1]))
        inv = lax.rsqrt(jnp.mean(a * a, axis=0, keepdims=True) + SUBLN_EPS)
        out_ref[...] = ((a * inv).T * subw_ref[...] * (1.0 - LAMBDA_INIT)).astype(out_ref.dtype)

    for sub in range(q_ref.shape[0] // t):
        tile = qi * (q_ref.shape[0] // t) + sub
        q_pos = (N_META + tile * t + lax.broadcasted_iota(jnp.int32, (1, t), 1)).astype(F32)
        attend(q_ref[sub * t:(sub + 1) * t, :], q_pos, tile, o_ref.at[sub * t:(sub + 1) * t, :])

    @pl.when(qi == 0)
    def _():
        pos = jnp.minimum(lax.broadcasted_iota(jnp.int32, (1, META_PAD), 1), N_META - 1)
        attend(qm_ref[...], pos.astype(F32), None, om_ref)


N_ATTN_INPUTS = 11
BF16_ROWS = 16


def _slab_rows(w, n_steps):
    n_rows = w.shape[0]
    for rows in range(BF16_ROWS, n_rows + 1, BF16_ROWS):
        if n_rows % rows == 0 and n_rows // rows <= n_steps:
            return rows
    return None


def _attention(lams, subw, y, y_meta_pad, cast_weights, *, t, tiles_per_step):
    b, seq, _ = y.shape
    tq = t * tiles_per_step
    n_q = seq // tq
    h_blk = N_HEADS
    vec = lambda n: pl.BlockSpec((1, n), lambda bi, h, i: (0, 0))

    def slab(w):
        rows = _slab_rows(w, b * N_HEADS * n_q)
        last = w.shape[0] // rows - 1
        return pl.BlockSpec((rows, w.shape[1]),
                            lambda bi, h, i: (jnp.minimum((bi * N_HEADS + h) * n_q + i, last), 0))

    acc_rows = V_DIM + SUM_ROWS
    return pl.pallas_call(
        functools.partial(_attn_kernel, t=t, n_cast=len(cast_weights)),
        grid=(b, N_HEADS, n_q),
        in_specs=[
            vec(HEAD_DIM), vec(HEAD_DIM), vec(HEAD_DIM), vec(HEAD_DIM), vec(V_DIM),
            pl.BlockSpec((None, tq, V_DIM), lambda bi, h, i: (bi, i, h)),
            pl.BlockSpec((META_PAD, V_DIM), lambda bi, h, i: (0, h)),
            pl.BlockSpec((None, seq, V_DIM), lambda bi, h, i: (bi, 0, h_blk + h)),
            pl.BlockSpec((None, seq, V_DIM), lambda bi, h, i: (bi, 0, 2 * h_blk + h)),
            pl.BlockSpec((META_PAD, V_DIM), lambda bi, h, i: (0, h_blk + h)),
            pl.BlockSpec((META_PAD, V_DIM), lambda bi, h, i: (0, 2 * h_blk + h)),
        ] + [slab(w) for w in cast_weights],
        out_specs=[
            pl.BlockSpec((None, tq, V_DIM), lambda bi, h, i: (bi, i, h)),
            pl.BlockSpec((None, META_PAD, V_DIM), lambda bi, h, i: (bi, 0, h)),
        ] + [slab(w) for w in cast_weights],
        out_shape=[
            jax.ShapeDtypeStruct((b, seq, N_HEADS * V_DIM), BF16),
            jax.ShapeDtypeStruct((b, META_PAD, N_HEADS * V_DIM), BF16),
        ] + [jax.ShapeDtypeStruct(w.shape, BF16) for w in cast_weights],
        scratch_shapes=[
            pltpu.VMEM((acc_rows, seq), BF16),
            pltpu.VMEM((acc_rows, META_PAD), BF16),
            pltpu.VMEM((3, t, t), F32),
            pltpu.VMEM((2, t, t), F32),
            pltpu.VMEM((2, t, t), F32),
            pltpu.VMEM((2, 8, t), F32),
            pltpu.VMEM((2, 8, t), F32),
            pltpu.VMEM((2, 8, t), F32),
            pltpu.VMEM((2, acc_rows, t), F32),
            pltpu.VMEM((8, V_DIM), F32),
        ],
        compiler_params=_params("arbitrary", "arbitrary", "arbitrary"),
        name="diff_attention",
    )(*[v.reshape(1, HEAD_DIM) for v in lams], subw.reshape(1, V_DIM),
      y, y_meta_pad, y, y, y_meta_pad, y_meta_pad, *cast_weights)


def _chan_dft_kernel(f_ref, cs_ref, o_ref):
    t = jnp.dot(f_ref[...], cs_ref[...], preferred_element_type=F32)
    o_ref[0] = t[:, :GROUP_DIM].astype(o_ref.dtype)
    o_ref[1] = t[:, GROUP_DIM:].astype(o_ref.dtype)


def _chan_dft(y, cs, *, tr):
    b, rows, in_width = y.shape
    f_blk0 = in_width // GROUP_DIM - N_GROUPS
    return pl.pallas_call(
        _chan_dft_kernel,
        grid=(b, rows // tr, N_GROUPS),
        in_specs=[
            pl.BlockSpec((None, tr, GROUP_DIM), lambda bi, r, g: (bi, r, f_blk0 + g)),
            pl.BlockSpec((GROUP_DIM, 2 * GROUP_DIM), lambda bi, r, g: (0, 0)),
        ],
        out_specs=pl.BlockSpec((None, 2, tr, GROUP_DIM), lambda bi, r, g: (bi, 0, r, g)),
        out_shape=jax.ShapeDtypeStruct((b, 2, rows, N_GROUPS * GROUP_DIM), BF16),
        compiler_params=_params("parallel", "parallel", "parallel"),
        name="chan_dft",
    )(y, cs)


def _pos_dft_kernel(mr_ref, mm_ref, t_ref, tm_ref, nw_ref, o_ref, *, scale):
    r = jnp.dot(mr_ref[...], t_ref[...], preferred_element_type=F32)
    r = r + jnp.dot(mm_ref[...], tm_ref[...], preferred_element_type=F32)
    r = r * scale
    inv = lax.rsqrt(jnp.mean(r * r, axis=-1, keepdims=True) + NORM_EPS)
    o_ref[...] = (r * inv * nw_ref[...]).astype(o_ref.dtype)


def _pos_dft(mr, mm, t, t_meta, nw, *, tp, scale):
    p_rows = mr.shape[0]
    b, k_real, width = t.shape
    k_meta = t_meta.shape[1]
    return pl.pallas_call(
        functools.partial(_pos_dft_kernel, scale=scale),
        grid=(p_rows // tp, b, N_GROUPS),
        in_specs=[
            pl.BlockSpec((tp, k_real), lambda p, bi, g: (p, 0)),
            pl.BlockSpec((tp, k_meta), lambda p, bi, g: (p, 0)),
            pl.BlockSpec((None, k_real, GROUP_DIM), lambda p, bi, g: (bi, 0, g)),
            pl.BlockSpec((None, k_meta, GROUP_DIM), lambda p, bi, g: (0, 0, g)),
            pl.BlockSpec((1, GROUP_DIM), lambda p, bi, g: (0, g)),
        ],
        out_specs=pl.BlockSpec((None, tp, GROUP_DIM), lambda p, bi, g: (bi, p, g)),
        out_shape=jax.ShapeDtypeStruct((b, p_rows, width), BF16),
        compiler_params=_params("parallel", "parallel", "arbitrary"),
        name="pos_dft",
    )(mr, mm, t, t_meta, nw.reshape(1, width))


TABLE_STEP = 64


def _dft_tables(seq):
    l_total = N_META + seq
    c = jnp.arange(GROUP_DIM, dtype=jnp.int32)
    ang_c = ((c[:, None] * c[None, :]) % GROUP_DIM).astype(F32) * (2.0 * math.pi / GROUP_DIM)
    cs = jnp.concatenate([jnp.cos(ang_c), jnp.sin(ang_c)], axis=1).astype(BF16)

    to_rad = 2.0 * math.pi / l_total
    half_pi = jnp.asarray([0.0, 0.5 * math.pi], F32)
    meta_p = jnp.arange(N_META, dtype=jnp.int32)
    real_p = N_META + jnp.arange(seq, dtype=jnp.int32)
    real_l, real_phase = jnp.tile(real_p, 2), jnp.repeat(half_pi, seq)
    meta_l, meta_phase = jnp.tile(meta_p, 2), jnp.repeat(half_pi, N_META)

    def angle(p, l, phase):
        return ((p[:, None] * l[None, :]) % l_total).astype(F32) * to_rad + phase[None, :]

    def direct(p, l, phase):
        return jnp.cos(angle(p, l, phase)).astype(BF16)

    hi = angle(N_META + TABLE_STEP * jnp.arange(seq // TABLE_STEP, dtype=jnp.int32), real_l, 0.0 * real_phase)
    lo = angle(jnp.arange(TABLE_STEP, dtype=jnp.int32), real_l, real_phase)
    real_rows_real_cols = (jnp.cos(hi)[:, None, :] * jnp.cos(lo)[None, :, :]
                           - jnp.sin(hi)[:, None, :] * jnp.sin(lo)[None, :, :]
                           ).reshape(seq, 2 * seq).astype(BF16)
    real_rows = (real_rows_real_cols, direct(real_p, meta_l, meta_phase))
    meta_rows = (direct(meta_p, real_l, real_phase), direct(meta_p, meta_l, meta_phase))
    return cs, real_rows, meta_rows


def _matmul_res_kernel(*refs, n_pairs):
    res_ref, o_ref = refs[2 * n_pairs], refs[2 * n_pairs + 1]
    acc = res_ref[...]
    for k in range(n_pairs):
        acc = acc + jnp.dot(refs[2 * k][...], refs[2 * k + 1][...], preferred_element_type=F32)
    o_ref[...] = acc


def _matmul_res(pairs, res, *, tm, tn):
    m, n = res.shape
    in_specs, args = [], []
    for a, w, blk in pairs:
        k = a.shape[1]
        in_specs += [pl.BlockSpec((tm, k), lambda i, j: (i, 0)),
                     pl.BlockSpec((k, tn), lambda i, j, blk=blk: (blk, j))]
        args += [a, w]
    in_specs.append(pl.BlockSpec((tm, tn), lambda i, j: (i, j)))
    return pl.pallas_call(
        functools.partial(_matmul_res_kernel, n_pairs=len(pairs)),
        grid=(m // tm, n // tn),
        in_specs=in_specs,
        out_specs=pl.BlockSpec((tm, tn), lambda i, j: (i, j)),
        out_shape=jax.ShapeDtypeStruct((m, n), F32),
        compiler_params=_params("parallel", "arbitrary"),
        name="matmul_res",
    )(*args, res)


HALO_ROWS = 8


def _up_glu_kernel(x_ref, nw_ref, wg_ref, wv_ref, halo_ref, cw_ref, cb_ref, o_ref, hn_ref, *, tm):
    @pl.when(pl.program_id(1) == 0)
    def _():
        def norm_rows(r, carry):
            rows = pl.ds(pl.multiple_of(r * NORM_ROWS, NORM_ROWS), NORM_ROWS)
            hn_ref[rows, :] = _norm_rows(x_ref, nw_ref, rows)
            return carry
        lax.fori_loop(0, tm // NORM_ROWS, norm_rows, 0)

    hn = hn_ref[...]
    g = jnp.dot(hn, wg_ref[...], preferred_element_type=F32)
    v = jnp.dot(hn, wv_ref[...], preferred_element_type=F32)
    row = lax.broadcasted_iota(jnp.int32, g.shape, 0)
    g_prev = jnp.where(row == 0, halo_ref[0, 0:1, :], pltpu.roll(g, 1, axis=0))
    g_next = jnp.where(row == tm - 1, halo_ref[0, 1:2, :], pltpu.roll(g, tm - 1, axis=0))
    conv = g_prev * cw_ref[0:1, :] + g * cw_ref[1:2, :] + g_next * cw_ref[2:3, :] + cb_ref[...]
    gelu = 0.5 * conv * (1.0 + lax.erf(conv * math.sqrt(0.5)))
    o_ref[...] = (gelu * v).astype(o_ref.dtype)


def _up_glu(x, nw, w_up, halo, conv_w, conv_b, *, tm, tc):
    m, d = x.shape
    d_ff = w_up.shape[1] // 2
    assert d_ff % LANES == 0 and d_ff >= tc
    n_c = pl.cdiv(d_ff, tc)
    el = pl.Element

    def col(c, base=0):
        return pl.multiple_of(base + jnp.minimum(c * tc, d_ff - tc), LANES)

    return pl.pallas_call(
        functools.partial(_up_glu_kernel, tm=tm),
        grid=(m // tm, n_c),
        in_specs=[
            pl.BlockSpec((tm, d), lambda i, c: (i, 0), pipeline_mode=pl.Buffered(1)),
            pl.BlockSpec((1, d), lambda i, c: (0, 0)),
            pl.BlockSpec((el(d), el(tc)), lambda i, c: (0, col(c))),
            pl.BlockSpec((el(d), el(tc)), lambda i, c: (0, col(c, d_ff))),
            pl.BlockSpec((el(1), el(HALO_ROWS), el(tc)), lambda i, c: (i, 0, col(c))),
            pl.BlockSpec((el(CONV_WIDTH), el(tc)), lambda i, c: (0, col(c))),
            pl.BlockSpec((el(1), el(tc)), lambda i, c: (0, col(c))),
        ],
        out_specs=pl.BlockSpec((el(tm), el(tc)), lambda i, c: (pl.multiple_of(i * tm, tm), col(c))),
        out_shape=jax.ShapeDtypeStruct((m, d_ff), BF16),
        scratch_shapes=[pltpu.VMEM((tm, d), BF16)],
        compiler_params=_params("parallel", "arbitrary"),
        name="up_glu",
    )(x, nw.reshape(1, d), w_up, w_up, halo, conv_w, conv_b.reshape(1, d_ff))


def _gate_halo(h_mid, h_mid_meta, nw, w_up, *, b, seq, tm, d_ff):
    d = h_mid.shape[-1]
    n_t = seq // tm
    h3 = h_mid.reshape(b, seq, d)
    pairs = [h3[:, k * tm - 1:k * tm + 1] for k in range(1, n_t)]
    meta_last = h_mid_meta.reshape(b, N_META, d)[:, N_META - 1:]
    rows = jnp.concatenate([meta_last] + pairs, axis=1).reshape(-1, d)
    n_rows = rows.shape[0]
    pad = -n_rows % 16
    rows = jnp.pad(rows, ((0, pad), (0, 0)))
    gates = _norm_matmul(rows, nw, w_up, tm=n_rows + pad, tn=256, n_cols=d_ff)[:n_rows]
    gates = gates.astype(F32).reshape(b, 2 * n_t - 1, d_ff)
    before = jnp.concatenate([gates[:, 0:1], gates[:, 1::2]], axis=1)
    after = jnp.concatenate([gates[:, 2::2], jnp.zeros((b, 1, d_ff), F32)], axis=1)
    halo = jnp.stack([before, after], axis=2)
    halo = jnp.pad(halo, ((0, 0), (0, 0), (0, HALO_ROWS - 2), (0, 0)))
    return halo.reshape(b * n_t, HALO_ROWS, d_ff)


def _rmsnorm_kernel(x_ref, w_ref, o_ref):
    x = x_ref[...]
    inv = lax.rsqrt(jnp.mean(x * x, axis=-1, keepdims=True) + NORM_EPS)
    o_ref[...] = x * inv * w_ref[...]


def _rmsnorm(x, w, *, tm):
    m, d = x.shape
    return pl.pallas_call(
        _rmsnorm_kernel,
        grid=(m // tm,),
        in_specs=[pl.BlockSpec((tm, d), lambda i: (i, 0)), pl.BlockSpec((1, d), lambda i: (0, 0))],
        out_specs=pl.BlockSpec((tm, d), lambda i: (i, 0)),
        out_shape=jax.ShapeDtypeStruct((m, d), F32),
        compiler_params=_params("parallel"),
        name="final_rmsnorm",
    )(x, w.reshape(1, d))


class _Tiles(NamedTuple):
    in_proj: tuple
    attention: tuple
    chan_dft: int
    pos_dft: int
    out_proj: tuple
    up_glu: tuple
    down_proj: tuple
    final_norm: int


def _tiles(seq, d):
    t_attn = min(512, seq)
    return _Tiles(in_proj=(512, 1024), attention=(t_attn, 8 if seq % (8 * t_attn) == 0 else 1),
                  chan_dft=min(1024, seq),
                  pos_dft=min(1024, seq), out_proj=(1024, min(1024, d)),
                  up_glu=(min(1024, seq), 512), down_proj=(512, min(512, d)), final_norm=512)


def kernel(x, meta_tokens, norm1_w, w_in, lambda_q1, lambda_k1, lambda_q2, lambda_k2,
           diff_subln_w, fourier_norm_w, w_out, norm2_w, w_up, conv_w, conv_b, w_down,
           final_norm_w):
    b, seq, d = x.shape
    m, n_meta_rows = b * seq, b * N_META
    in_width = w_in.shape[-1]
    d_ff = w_down.shape[1]
    assert w_in.shape[0] == 1 and meta_tokens.shape[0] == N_META
    tiles = _tiles(seq, d)

    w_in_b = w_in[0].astype(BF16)
    lams = (lambda_q1[0], lambda_k1[0], lambda_q2[0], lambda_k2[0])

    later = (w_out[0], w_up[0], w_down[0])
    t_attn, tiles_per_step = tiles.attention
    attn_steps = b * N_HEADS * (seq // (t_attn * tiles_per_step))
    in_attention = all(_slab_rows(w, attn_steps) for w in later)

    tm, tn = tiles.in_proj
    y = _norm_matmul(x.reshape(m, d), norm1_w[0], w_in_b, tm=tm, tn=tn).reshape(b, seq, in_width)
    y_meta = _norm_matmul(meta_tokens, norm1_w[0], w_in_b, tm=N_META, tn=tn)

    y_meta_pad = jnp.pad(y_meta, ((0, META_PAD - N_META), (0, 0)))
    attn, attn_meta, *cast = _attention(lams, diff_subln_w[0], y, y_meta_pad,
                                        later if in_attention else (), t=t_attn,
                                        tiles_per_step=tiles_per_step)
    attn_meta = attn_meta[:, :N_META]
    w_out_b, w_up_b, w_down_b = cast if in_attention else (w.astype(BF16) for w in later)

    cs, real_rows, meta_rows = _dft_tables(seq)
    four_w = N_GROUPS * GROUP_DIM
    t = _chan_dft(y, cs, tr=tiles.chan_dft).reshape(b, 2 * seq, four_w)
    t_meta = _chan_dft(y_meta[None], cs, tr=N_META).reshape(1, 2 * N_META, four_w)
    scale = 1.0 / math.sqrt((N_META + seq) * GROUP_DIM)
    four = _pos_dft(*real_rows, t, t_meta, fourier_norm_w[0], tp=tiles.pos_dft, scale=scale)
    four_meta = _pos_dft(*meta_rows, t, t_meta, fourier_norm_w[0], tp=N_META, scale=scale)

    attn_w = attn.shape[-1]
    tm, tn = tiles.out_proj
    h_mid = _matmul_res([(attn.reshape(m, attn_w), w_out_b, 0), (four.reshape(m, four_w), w_out_b, 1)],
                        x.reshape(m, d), tm=tm, tn=tn)
    meta_res = jnp.broadcast_to(meta_tokens[None], (b, N_META, d)).reshape(n_meta_rows, d)
    h_mid_meta = _matmul_res([(attn_meta.reshape(n_meta_rows, attn_w), w_out_b, 0),
                              (four_meta.reshape(n_meta_rows, four_w), w_out_b, 1)],
                             meta_res, tm=n_meta_rows, tn=tn)

    tm, tc = tiles.up_glu
    halo = _gate_halo(h_mid, h_mid_meta, norm2_w[0], w_up_b, b=b, seq=seq, tm=tm, d_ff=d_ff)
    z = _up_glu(h_mid, norm2_w[0], w_up_b, halo, conv_w[0], conv_b[0], tm=tm, tc=tc)
    tm, tn = tiles.down_proj
    h_out = _matmul_res([(z, w_down_b, 0)], h_mid, tm=tm, tn=tn)

    return _rmsnorm(h_out, final_norm_w, tm=tiles.final_norm).reshape(b, seq, d)
```

```python
import functools
import math
from typing import NamedTuple

import jax
import jax.numpy as jnp
from jax import lax
from jax.experimental import pallas as pl
from jax.experimental.pallas import tpu as pltpu

N_META = 16
N_HEADS = 16
HEAD_DIM = 64
V_DIM = 128
LANES = 128
N_GROUPS = 4
GROUP_DIM = 512
CONV_WIDTH = 3
NORM_EPS = 1e-6
SUBLN_EPS = 1e-5
LAMBDA_INIT = 0.8 - 0.6 * math.exp(-0.3 * 0)

VMEM_LIMIT_BYTES = 56 * 1024 * 1024
NORM_ROWS = 256
BF16 = jnp.bfloat16
F32 = jnp.float32


def _params(*sem):
    return pltpu.CompilerParams(dimension_semantics=sem, vmem_limit_bytes=VMEM_LIMIT_BYTES)


def _norm_rows(x_ref, nw_ref, rows):
    x = x_ref[rows, :]
    inv = lax.rsqrt(jnp.mean(x * x, axis=-1, keepdims=True) + NORM_EPS)
    return (x * inv * nw_ref[...]).astype(BF16)


def _norm_matmul_kernel(x_ref, nw_ref, w_ref, o_ref, hn_ref, *, row_chunks):
    @pl.when(pl.program_id(1) == 0)
    def _():
        for rows in row_chunks:
            hn = _norm_rows(x_ref, nw_ref, rows)
            hn_ref[rows, :] = hn
            o_ref[rows, :] = jnp.dot(hn, w_ref[...], preferred_element_type=F32).astype(o_ref.dtype)

    @pl.when(pl.program_id(1) != 0)
    def _():
        o_ref[...] = jnp.dot(hn_ref[...], w_ref[...], preferred_element_type=F32).astype(o_ref.dtype)


def _norm_matmul(x, nw, w, *, tm, tn, n_cols=None):
    m, d = x.shape
    n = w.shape[1] if n_cols is None else n_cols
    step = min(tm, NORM_ROWS)
    row_chunks = tuple(slice(r, r + step) for r in range(0, tm, step))
    return pl.pallas_call(
        functools.partial(_norm_matmul_kernel, row_chunks=row_chunks),
        grid=(m // tm, n // tn),
        in_specs=[
            pl.BlockSpec((tm, d), lambda i, j: (i, 0)),
            pl.BlockSpec((1, d), lambda i, j: (0, 0)),
            pl.BlockSpec((d, tn), lambda i, j: (0, j)),
        ],
        out_specs=pl.BlockSpec((tm, tn), lambda i, j: (i, j)),
        out_shape=jax.ShapeDtypeStruct((m, n), BF16),
        scratch_shapes=[pltpu.VMEM((tm, d), BF16)],
        compiler_params=_params("parallel", "arbitrary"),
        name="norm_matmul",
    )(x, nw.reshape(1, d), w)


META_PAD = 128
SUM_ROWS = 16
LOG2E = 1.0 / math.log(2.0)
BEFORE, DIAG, AFTER = 0, 1, 2
SKIP_BITS = 160.0
NORM_SLACK = 1.01


def _attn_kernel(*refs, t, n_cast):
    (lq1_ref, lk1_ref, lq2_ref, lk2_ref, subw_ref, q_ref, qm_ref, k_ref, v_ref, km_ref,
     vm_ref) = refs[:N_ATTN_INPUTS]
    cast_in = refs[N_ATTN_INPUTS:N_ATTN_INPUTS + n_cast]
    o_ref, om_ref = refs[N_ATTN_INPUTS + n_cast:N_ATTN_INPUTS + n_cast + 2]
    cast_out = refs[N_ATTN_INPUTS + n_cast + 2:N_ATTN_INPUTS + 2 * n_cast + 2]
    (vt_ref, vtm_ref, bias_ref, ta_ref, tb_ref, ma_ref, mb_ref, m_ref, acc_ref,
     knorm_ref) = refs[N_ATTN_INPUTS + 2 * n_cast + 2:]
    for w_ref, wb_ref in zip(cast_in, cast_out):
        wb_ref[...] = w_ref[...].astype(wb_ref.dtype)

    h = pl.program_id(1)
    qi = pl.program_id(2)
    seq = k_ref.shape[0]
    n_chunks = seq // t
    assert n_chunks >= 2, "every query tile needs a neighbouring key chunk"
    acc_rows = V_DIM + SUM_ROWS

    slope = LOG2E * jnp.exp2(jnp.full((1, 1), -8.0 / N_HEADS, F32) * jnp.asarray(h + 1, F32))
    lane = lax.broadcasted_iota(jnp.int32, (1, V_DIM), 1)
    map_cols = (lax.broadcasted_iota(jnp.int32, (V_DIM, V_DIM), 0) // HEAD_DIM
                == lax.broadcasted_iota(jnp.int32, (V_DIM, V_DIM), 1)).astype(BF16)
    map_rows = (lax.broadcasted_iota(jnp.int32, (8, V_DIM), 1) // HEAD_DIM
                == lax.broadcasted_iota(jnp.int32, (8, V_DIM), 0)).astype(BF16)

    @pl.when(qi == 0)
    def _():
        for c in range(n_chunks):
            vt_ref[0:V_DIM, c * t:(c + 1) * t] = v_ref[c * t:(c + 1) * t, :].astype(F32).T.astype(BF16)
        vt_ref[V_DIM:acc_rows, :] = jnp.ones((SUM_ROWS, seq), BF16)
        vtm_ref[0:V_DIM, :] = vm_ref[...].astype(F32).T.astype(BF16)
        vtm_ref[V_DIM:acc_rows, :] = jnp.ones((SUM_ROWS, META_PAD), BF16)
        row = lax.broadcasted_iota(jnp.int32, (t, t), 0)
        col = lax.broadcasted_iota(jnp.int32, (t, t), 1)
        bias_ref[BEFORE] = slope * row.astype(F32)
        bias_ref[DIAG] = -slope * jnp.abs(col - row).astype(F32)
        bias_ref[AFTER] = -slope * row.astype(F32)
        k_sq = jnp.square(k_ref[...].astype(F32)).astype(BF16)
        sq_norms = jnp.dot(k_sq, map_cols, preferred_element_type=F32)
        knorm_ref[0:1, :] = jnp.sqrt(jnp.max(sq_norms, axis=0, keepdims=True))

    lam = (jnp.exp(jnp.sum(lq1_ref[...] * lk1_ref[...], axis=-1, keepdims=True))
           - jnp.exp(jnp.sum(lq2_ref[...] * lk2_ref[...], axis=-1, keepdims=True))
           + LAMBDA_INIT)
    meta_row = lax.broadcasted_iota(jnp.int32, (N_META, 1), 0)

    def attend(q, q_pos, tile_idx, out_ref):
        nq = q.shape[0]
        q32 = q.astype(F32) * (LOG2E / math.sqrt(HEAD_DIM))
        qs = (jnp.where(lane < HEAD_DIM, q32, 0.0).astype(BF16),
              jnp.where(lane >= HEAD_DIM, q32, 0.0).astype(BF16))

        def scores(k_c):
            return tuple(lax.dot_general(k_c, qc, (((1,), (1,)), ((), ())), preferred_element_type=F32)
                         for qc in qs)

        def update(c, tt, tt_max, row_bias, vt_c):
            m = m_ref[c, 0:1, 0:nq]
            m_new = jnp.maximum(m, tt_max + row_bias)
            p = jnp.exp2(tt - (m_new - row_bias))
            acc_ref[c, :, 0:nq] = (jnp.exp2(m - m_new) * acc_ref[c, :, 0:nq]
                                   + jnp.dot(vt_c, p.astype(BF16), preferred_element_type=F32))
            m_ref[c, 0:1, 0:nq] = m_new

        def produce(c_idx, kind, buf_ref, max_ref):
            k_c = k_ref[pl.ds(pl.multiple_of(c_idx * t, t), t), :]
            key_bias = bias_ref[kind, :, 0:nq]
            for c, s in enumerate(scores(k_c)):
                tt = s + key_bias
                buf_ref[c, :, 0:nq] = tt
                max_ref[c, 0:1, 0:nq] = jnp.max(tt, axis=0, keepdims=True)

        def consume(c_idx, kind, buf_ref, max_ref):
            sign = 1.0 - kind
            k_pos0 = jnp.asarray(N_META + c_idx * t, F32)
            row_bias = (-sign) * slope * (q_pos - k_pos0)
            vt_c = vt_ref[:, pl.ds(pl.multiple_of(c_idx * t, t), t)]
            for c in range(2):
                update(c, buf_ref[c, :, 0:nq], max_ref[c, 0:1, 0:nq], row_bias, vt_c)

        m_ref[...] = jnp.full(m_ref.shape, -jnp.inf, F32)
        acc_ref[...] = jnp.zeros(acc_ref.shape, F32)
        buf_a, buf_b = (ta_ref, ma_ref), (tb_ref, mb_ref)
        if tile_idx is None:
            first = (jnp.int32(0), AFTER)
        else:
            produce(tile_idx, DIAG, *buf_a)
            has_before = tile_idx > 0
            first = (jnp.where(has_before, tile_idx - 1, tile_idx + 1),
                     jnp.where(has_before, BEFORE, AFTER))

        if tile_idx is None:
            key_bias, row_bias = -slope * jnp.abs(q_pos - meta_row.astype(F32)), jnp.zeros((1, nq), F32)
        else:
            key_bias, row_bias = bias_ref[BEFORE, 0:N_META, 0:nq], -slope * q_pos
        for c, s in enumerate(scores(km_ref[0:N_META, :])):
            tt = s + key_bias
            update(c, tt, jnp.max(tt, axis=0, keepdims=True), row_bias, vtm_ref[:, 0:N_META])

        produce(*first, *buf_b)
        if tile_idx is not None:
            consume(tile_idx, DIAG, *buf_a)

        q_sq = jnp.square(q32.astype(BF16).astype(F32)).astype(BF16)
        q_norm = jnp.sqrt(lax.dot_general(map_rows, q_sq, (((1,), (1,)), ((), ())),
                                          preferred_element_type=F32))
        bound = jnp.full((1, 1), -jnp.inf, F32)
        for c in range(2):
            gap = knorm_ref[0:1, c:c + 1] * q_norm[c:c + 1] * NORM_SLACK - m_ref[c, 0:1, 0:nq]
            bound = jnp.maximum(bound, jnp.max(gap, axis=-1, keepdims=True))
        reach = ((bound + SKIP_BITS) / slope - 1.0) / t + 1.0
        n_side = jnp.max(jnp.clip(jnp.floor(reach), 1.0, float(n_chunks))).astype(jnp.int32)
        if tile_idx is None:
            n_before, n_todo = 0, jnp.minimum(n_chunks, n_side)
        else:
            n_before = jnp.minimum(tile_idx, n_side)
            n_todo = n_before + jnp.minimum(n_chunks - 1 - tile_idx, n_side)

        def todo(k):
            if tile_idx is None:
                return k, AFTER
            before = k < n_before
            c_idx = jnp.where(before, tile_idx - 1 - k, tile_idx + 1 + k - n_before)
            return c_idx, jnp.where(before, BEFORE, AFTER)

        def pair(j, carry):
            k = 2 * j
            produce(*todo(k + 1), *buf_a)
            consume(*todo(k), *buf_b)
            produce(*todo(k + 2), *buf_b)
            consume(*todo(k + 1), *buf_a)
            return carry

        n_pairs = (n_todo - 1) // 2
        lax.fori_loop(0, n_pairs, pair, 0)

        @pl.when(n_todo - 2 * n_pairs == 2)
        def _():
            produce(*todo(n_todo - 1), *buf_a)
            consume(*todo(n_todo - 2), *buf_b)
            consume(*todo(n_todo - 1), *buf_a)

        @pl.when(n_todo - 2 * n_pairs == 1)
        def _():
            consume(*todo(n_todo - 1), *buf_b)

        acc1, acc2 = acc_ref[0, :, 0:nq], acc_ref[1, :, 0:nq]
        a = (acc1[0:V_DIM] * (1.0 / acc1[V_DIM:V_DIM + 1])
             - acc2[0:V_DIM] * (lam / acc2[V_DIM:V_DIM + 1]))
        inv = lax.rsqrt(jnp.mean(a * a, axis=0, keepdims=True) + SUBLN_EPS)
        out_ref[...] = ((a * inv).T * subw_ref[...] * (1.0 - LAMBDA_INIT)).astype(out_ref.dtype)

    for sub in range(q_ref.shape[0] // t):
        tile = qi * (q_ref.shape[0] // t) + sub
        q_pos = (N_META + tile * t + lax.broadcasted_iota(jnp.int32, (1, t), 1)).astype(F32)
        attend(q_ref[sub * t:(sub + 1) * t, :], q_pos, tile, o_ref.at[sub * t:(sub + 1) * t, :])

    @pl.when(qi == 0)
    def _():
        pos = jnp.minimum(lax.broadcasted_iota(jnp.int32, (1, META_PAD), 1), N_META - 1)
        attend(qm_ref[...], pos.astype(F32), None, om_ref)


N_ATTN_INPUTS = 11
BF16_ROWS = 16


def _slab_rows(w, n_steps):
    n_rows = w.shape[0]
    for rows in range(BF16_ROWS, n_rows + 1, BF16_ROWS):
        if n_rows % rows == 0 and n_rows // rows <= n_steps:
            return rows
    return None


def _attention(lams, subw, y, y_meta_pad, cast_weights, *, t, tiles_per_step):
    b, seq, _ = y.shape
    tq = t * tiles_per_step
    n_q = seq // tq
    h_blk = N_HEADS
    vec = lambda n: pl.BlockSpec((1, n), lambda bi, h, i: (0, 0))

    def slab(w):
        rows = _slab_rows(w, b * N_HEADS * n_q)
        last = w.shape[0] // rows - 1
        return pl.BlockSpec((rows, w.shape[1]),
                            lambda bi, h, i: (jnp.minimum((bi * N_HEADS + h) * n_q + i, last), 0))

    acc_rows = V_DIM + SUM_ROWS
    return pl.pallas_call(
        functools.partial(_attn_kernel, t=t, n_cast=len(cast_weights)),
        grid=(b, N_HEADS, n_q),
        in_specs=[
            vec(HEAD_DIM), vec(HEAD_DIM), vec(HEAD_DIM), vec(HEAD_DIM), vec(V_DIM),
            pl.BlockSpec((None, tq, V_DIM), lambda bi, h, i: (bi, i, h)),
            pl.BlockSpec((META_PAD, V_DIM), lambda bi, h, i: (0, h)),
            pl.BlockSpec((None, seq, V_DIM), lambda bi, h, i: (bi, 0, h_blk + h)),
            pl.BlockSpec((None, seq, V_DIM), lambda bi, h, i: (bi, 0, 2 * h_blk + h)),
            pl.BlockSpec((META_PAD, V_DIM), lambda bi, h, i: (0, h_blk + h)),
            pl.BlockSpec((META_PAD, V_DIM), lambda bi, h, i: (0, 2 * h_blk + h)),
        ] + [slab(w) for w in cast_weights],
        out_specs=[
            pl.BlockSpec((None, tq, V_DIM), lambda bi, h, i: (bi, i, h)),
            pl.BlockSpec((None, META_PAD, V_DIM), lambda bi, h, i: (bi, 0, h)),
        ] + [slab(w) for w in cast_weights],
        out_shape=[
            jax.ShapeDtypeStruct((b, seq, N_HEADS * V_DIM), BF16),
            jax.ShapeDtypeStruct((b, META_PAD, N_HEADS * V_DIM), BF16),
        ] + [jax.ShapeDtypeStruct(w.shape, BF16) for w in cast_weights],
        scratch_shapes=[
            pltpu.VMEM((acc_rows, seq), BF16),
            pltpu.VMEM((acc_rows, META_PAD), BF16),
            pltpu.VMEM((3, t, t), F32),
            pltpu.VMEM((2, t, t), F32),
            pltpu.VMEM((2, t, t), F32),
            pltpu.VMEM((2, 8, t), F32),
            pltpu.VMEM((2, 8, t), F32),
            pltpu.VMEM((2, 8, t), F32),
            pltpu.VMEM((2, acc_rows, t), F32),
            pltpu.VMEM((8, V_DIM), F32),
        ],
        compiler_params=_params("arbitrary", "arbitrary", "arbitrary"),
        name="diff_attention",
    )(*[v.reshape(1, HEAD_DIM) for v in lams], subw.reshape(1, V_DIM),
      y, y_meta_pad, y, y, y_meta_pad, y_meta_pad, *cast_weights)


def _chan_dft_kernel(f_ref, cs_ref, o_ref):
    t = jnp.dot(f_ref[...], cs_ref[...], preferred_element_type=F32)
    o_ref[0] = t[:, :GROUP_DIM].astype(o_ref.dtype)
    o_ref[1] = t[:, GROUP_DIM:].astype(o_ref.dtype)


def _chan_dft(y, cs, *, tr):
    b, rows, in_width = y.shape
    f_blk0 = in_width // GROUP_DIM - N_GROUPS
    return pl.pallas_call(
        _chan_dft_kernel,
        grid=(b, rows // tr, N_GROUPS),
        in_specs=[
            pl.BlockSpec((None, tr, GROUP_DIM), lambda bi, r, g: (bi, r, f_blk0 + g)),
            pl.BlockSpec((GROUP_DIM, 2 * GROUP_DIM), lambda bi, r, g: (0, 0)),
        ],
        out_specs=pl.BlockSpec((None, 2, tr, GROUP_DIM), lambda bi, r, g: (bi, 0, r, g)),
        out_shape=jax.ShapeDtypeStruct((b, 2, rows, N_GROUPS * GROUP_DIM), BF16),
        compiler_params=_params("parallel", "parallel", "parallel"),
        name="chan_dft",
    )(y, cs)


def _pos_dft_kernel(mr_ref, mm_ref, t_ref, tm_ref, nw_ref, o_ref, *, scale):
    r = jnp.dot(mr_ref[...], t_ref[...], preferred_element_type=F32)
    r = r + jnp.dot(mm_ref[...], tm_ref[...], preferred_element_type=F32)
    r = r * scale
    inv = lax.rsqrt(jnp.mean(r * r, axis=-1, keepdims=True) + NORM_EPS)
    o_ref[...] = (r * inv * nw_ref[...]).astype(o_ref.dtype)


def _pos_dft(mr, mm, t, t_meta, nw, *, tp, scale):
    p_rows = mr.shape[0]
    b, k_real, width = t.shape
    k_meta = t_meta.shape[1]
    return pl.pallas_call(
        functools.partial(_pos_dft_kernel, scale=scale),
        grid=(p_rows // tp, b, N_GROUPS),
        in_specs=[
            pl.BlockSpec((tp, k_real), lambda p, bi, g: (p, 0)),
            pl.BlockSpec((tp, k_meta), lambda p, bi, g: (p, 0)),
            pl.BlockSpec((None, k_real, GROUP_DIM), lambda p, bi, g: (bi, 0, g)),
            pl.BlockSpec((None, k_meta, GROUP_DIM), lambda p, bi, g: (0, 0, g)),
            pl.BlockSpec((1, GROUP_DIM), lambda p, bi, g: (0, g)),
        ],
        out_specs=pl.BlockSpec((None, tp, GROUP_DIM), lambda p, bi, g: (bi, p, g)),
        out_shape=jax.ShapeDtypeStruct((b, p_rows, width), BF16),
        compiler_params=_params("parallel", "parallel", "arbitrary"),
        name="pos_dft",
    )(mr, mm, t, t_meta, nw.reshape(1, width))


TABLE_STEP = 64


def _dft_tables(seq):
    l_total = N_META + seq
    c = jnp.arange(GROUP_DIM, dtype=jnp.int32)
    ang_c = ((c[:, None] * c[None, :]) % GROUP_DIM).astype(F32) * (2.0 * math.pi / GROUP_DIM)
    cs = jnp.concatenate([jnp.cos(ang_c), jnp.sin(ang_c)], axis=1).astype(BF16)

    to_rad = 2.0 * math.pi / l_total
    half_pi = jnp.asarray([0.0, 0.5 * math.pi], F32)
    meta_p = jnp.arange(N_META, dtype=jnp.int32)
    real_p = N_META + jnp.arange(seq, dtype=jnp.int32)
    real_l, real_phase = jnp.tile(real_p, 2), jnp.repeat(half_pi, seq)
    meta_l, meta_phase = jnp.tile(meta_p, 2), jnp.repeat(half_pi, N_META)

    def angle(p, l, phase):
        return ((p[:, None] * l[None, :]) % l_total).astype(F32) * to_rad + phase[None, :]

    def direct(p, l, phase):
        return jnp.cos(angle(p, l, phase)).astype(BF16)

    hi = angle(N_META + TABLE_STEP * jnp.arange(seq // TABLE_STEP, dtype=jnp.int32), real_l, 0.0 * real_phase)
    lo = angle(jnp.arange(TABLE_STEP, dtype=jnp.int32), real_l, real_phase)
    real_rows_real_cols = (jnp.cos(hi)[:, None, :] * jnp.cos(lo)[None, :, :]
                           - jnp.sin(hi)[:, None, :] * jnp.sin(lo)[None, :, :]
                           ).reshape(seq, 2 * seq).astype(BF16)
    real_rows = (real_rows_real_cols, direct(real_p, meta_l, meta_phase))
    meta_rows = (direct(meta_p, real_l, real_phase), direct(meta_p, meta_l, meta_phase))
    return cs, real_rows, meta_rows


def _matmul_res_kernel(*refs, n_pairs):
    res_ref, o_ref = refs[2 * n_pairs], refs[2 * n_pairs + 1]
    acc = res_ref[...]
    for k in range(n_pairs):
        acc = acc + jnp.dot(refs[2 * k][...], refs[2 * k + 1][...], preferred_element_type=F32)
    o_ref[...] = acc


def _matmul_res(pairs, res, *, tm, tn):
    m, n = res.shape
    in_specs, args = [], []
    for a, w, blk, k in pairs:
        in_specs += [pl.BlockSpec((tm, k), lambda i, j: (i, 0)),
                     pl.BlockSpec((k, tn), lambda i, j, blk=blk: (blk, j))]
        args += [a, w]
    in_specs.append(pl.BlockSpec((tm, tn), lambda i, j: (i, j)))
    return pl.pallas_call(
        functools.partial(_matmul_res_kernel, n_pairs=len(pairs)),
        grid=(m // tm, n // tn),
        in_specs=in_specs,
        out_specs=pl.BlockSpec((tm, tn), lambda i, j: (i, j)),
        out_shape=jax.ShapeDtypeStruct((m, n), F32),
        compiler_params=_params("parallel", "arbitrary"),
        name="matmul_res",
    )(*args, res)


HALO_ROWS = 8


def _up_glu_kernel(x_hbm, nw_ref, wg_ref, wv_ref, halo_ref, cw_ref, cb_ref, o_ref, hn_ref,
                   x_ref, x_sem, *, tm, tail):
    i, c = pl.program_id(0), pl.program_id(1)

    def x_copy(tile):
        rows = pl.ds(pl.multiple_of(tile * tm, tm), tm)
        return pltpu.make_async_copy(x_hbm.at[rows, :], x_ref, x_sem)

    @pl.when((i == 0) & (c == 0))
    def _():
        x_copy(i).start()

    @pl.when((c == 1) & (i + 1 < pl.num_programs(0)))
    def _():
        x_copy(i + 1).start()

    @pl.when(c == 0)
    def _():
        x_copy(i).wait()

        def norm_rows(r, carry):
            rows = pl.ds(pl.multiple_of(r * NORM_ROWS, NORM_ROWS), NORM_ROWS)
            hn_ref[rows, :] = _norm_rows(x_ref, nw_ref, rows)
            return carry
        lax.fori_loop(0, tm // NORM_ROWS, norm_rows, 0)

    hn = hn_ref[...]
    g = jnp.dot(hn, wg_ref[...], preferred_element_type=F32)
    v = jnp.dot(hn, wv_ref[...], preferred_element_type=F32)
    row = lax.broadcasted_iota(jnp.int32, g.shape, 0)
    g_prev = jnp.where(row == 0, halo_ref[0, 0:1, :], pltpu.roll(g, 1, axis=0))
    g_next = jnp.where(row == tm - 1, halo_ref[0, 1:2, :], pltpu.roll(g, tm - 1, axis=0))
    conv = g_prev * cw_ref[0:1, :] + g * cw_ref[1:2, :] + g_next * cw_ref[2:3, :] + cb_ref[...]
    gelu = 0.5 * conv * (1.0 + lax.erf(conv * math.sqrt(0.5)))
    z = (gelu * v).astype(o_ref.dtype)
    tc = z.shape[1]
    if tail == tc:
        o_ref[...] = z
    else:
        @pl.when(c < pl.num_programs(1) - 1)
        def _():
            o_ref[...] = z

        @pl.when(c == pl.num_programs(1) - 1)
        def _():
            o_ref[:, 0:tail] = z[:, tc - tail:]
            o_ref[:, tail:] = jnp.zeros((tm, tc - tail), o_ref.dtype)


def _up_glu(x, nw, w_up, halo, conv_w, conv_b, *, tm, tc):
    m, d = x.shape
    d_ff = w_up.shape[1] // 2
    assert d_ff % LANES == 0 and d_ff >= tc
    n_c = pl.cdiv(d_ff, tc)
    assert n_c >= 2, "the next row tile is prefetched at column step 1"
    tail = d_ff - (n_c - 1) * tc
    el = pl.Element

    def col(c, base=0):
        return pl.multiple_of(base + jnp.minimum(c * tc, d_ff - tc), LANES)

    return pl.pallas_call(
        functools.partial(_up_glu_kernel, tm=tm, tail=tail),
        grid=(m // tm, n_c),
        in_specs=[
            pl.BlockSpec(memory_space=pl.ANY),
            pl.BlockSpec((1, d), lambda i, c: (0, 0)),
            pl.BlockSpec((el(d), el(tc)), lambda i, c: (0, col(c))),
            pl.BlockSpec((el(d), el(tc)), lambda i, c: (0, col(c, d_ff))),
            pl.BlockSpec((el(1), el(HALO_ROWS), el(tc)), lambda i, c: (i, 0, col(c))),
            pl.BlockSpec((el(CONV_WIDTH), el(tc)), lambda i, c: (0, col(c))),
            pl.BlockSpec((el(1), el(tc)), lambda i, c: (0, col(c))),
        ],
        out_specs=pl.BlockSpec((tm, tc), lambda i, c: (i, c)),
        out_shape=jax.ShapeDtypeStruct((m, n_c * tc), BF16),
        scratch_shapes=[pltpu.VMEM((tm, d), BF16), pltpu.VMEM((tm, d), F32),
                        pltpu.SemaphoreType.DMA(())],
        compiler_params=_params("arbitrary", "arbitrary"),
        name="up_glu",
    )(x, nw.reshape(1, d), w_up, w_up, halo, conv_w, conv_b.reshape(1, d_ff))


def _gate_halo(h_mid, h_mid_meta, nw, w_up, *, b, seq, tm, d_ff):
    d = h_mid.shape[-1]
    n_t = seq // tm
    h3 = h_mid.reshape(b, seq, d)
    pairs = [h3[:, k * tm - 1:k * tm + 1] for k in range(1, n_t)]
    meta_last = h_mid_meta.reshape(b, N_META, d)[:, N_META - 1:]
    rows = jnp.concatenate([meta_last] + pairs, axis=1).reshape(-1, d)
    n_rows = rows.shape[0]
    pad = -n_rows % 16
    rows = jnp.pad(rows, ((0, pad), (0, 0)))
    gates = _norm_matmul(rows, nw, w_up, tm=n_rows + pad, tn=256, n_cols=d_ff)[:n_rows]
    gates = gates.astype(F32).reshape(b, 2 * n_t - 1, d_ff)
    before = jnp.concatenate([gates[:, 0:1], gates[:, 1::2]], axis=1)
    after = jnp.concatenate([gates[:, 2::2], jnp.zeros((b, 1, d_ff), F32)], axis=1)
    halo = jnp.stack([before, after], axis=2)
    halo = jnp.pad(halo, ((0, 0), (0, 0), (0, HALO_ROWS - 2), (0, 0)))
    return halo.reshape(b * n_t, HALO_ROWS, d_ff)


def _rmsnorm_kernel(x_ref, w_ref, o_ref):
    x = x_ref[...]
    inv = lax.rsqrt(jnp.mean(x * x, axis=-1, keepdims=True) + NORM_EPS)
    o_ref[...] = x * inv * w_ref[...]


def _rmsnorm(x, w, *, tm):
    m, d = x.shape
    return pl.pallas_call(
        _rmsnorm_kernel,
        grid=(m // tm,),
        in_specs=[pl.BlockSpec((tm, d), lambda i: (i, 0)), pl.BlockSpec((1, d), lambda i: (0, 0))],
        out_specs=pl.BlockSpec((tm, d), lambda i: (i, 0)),
        out_shape=jax.ShapeDtypeStruct((m, d), F32),
        compiler_params=_params("parallel"),
        name="final_rmsnorm",
    )(x, w.reshape(1, d))


class _Tiles(NamedTuple):
    in_proj: tuple
    attention: tuple
    chan_dft: int
    pos_dft: int
    out_proj: tuple
    up_glu: tuple
    down_proj: tuple
    final_norm: int


def _tiles(seq, d):
    t_attn = min(512, seq)
    return _Tiles(in_proj=(512, 1024), attention=(t_attn, 8 if seq % (8 * t_attn) == 0 else 1),
                  chan_dft=min(1024, seq),
                  pos_dft=min(1024, seq), out_proj=(1024, min(1024, d)),
                  up_glu=(min(1024, seq), 512), down_proj=(512, min(512, d)), final_norm=512)


def kernel(x, meta_tokens, norm1_w, w_in, lambda_q1, lambda_k1, lambda_q2, lambda_k2,
           diff_subln_w, fourier_norm_w, w_out, norm2_w, w_up, conv_w, conv_b, w_down,
           final_norm_w):
    b, seq, d = x.shape
    m, n_meta_rows = b * seq, b * N_META
    in_width = w_in.shape[-1]
    d_ff = w_down.shape[1]
    assert w_in.shape[0] == 1 and meta_tokens.shape[0] == N_META
    tiles = _tiles(seq, d)

    w_in_b = w_in[0].astype(BF16)
    lams = (lambda_q1[0], lambda_k1[0], lambda_q2[0], lambda_k2[0])

    later = (w_out[0], w_up[0], w_down[0])
    t_attn, tiles_per_step = tiles.attention
    attn_steps = b * N_HEADS * (seq // (t_attn * tiles_per_step))
    in_attention = all(_slab_rows(w, attn_steps) for w in later)

    tm, tn = tiles.in_proj
    y = _norm_matmul(x.reshape(m, d), norm1_w[0], w_in_b, tm=tm, tn=tn).reshape(b, seq, in_width)
    y_meta = _norm_matmul(meta_tokens, norm1_w[0], w_in_b, tm=N_META, tn=tn)

    y_meta_pad = jnp.pad(y_meta, ((0, META_PAD - N_META), (0, 0)))
    attn, attn_meta, *cast = _attention(lams, diff_subln_w[0], y, y_meta_pad,
                                        later if in_attention else (), t=t_attn,
                                        tiles_per_step=tiles_per_step)
    attn_meta = attn_meta[:, :N_META]
    w_out_b, w_up_b, w_down_b = cast if in_attention else (w.astype(BF16) for w in later)

    cs, real_rows, meta_rows = _dft_tables(seq)
    four_w = N_GROUPS * GROUP_DIM
    t = _chan_dft(y, cs, tr=tiles.chan_dft).reshape(b, 2 * seq, four_w)
    t_meta = _chan_dft(y_meta[None], cs, tr=N_META).reshape(1, 2 * N_META, four_w)
    scale = 1.0 / math.sqrt((N_META + seq) * GROUP_DIM)
    four = _pos_dft(*real_rows, t, t_meta, fourier_norm_w[0], tp=tiles.pos_dft, scale=scale)
    four_meta = _pos_dft(*meta_rows, t, t_meta, fourier_norm_w[0], tp=N_META, scale=scale)

    attn_w = attn.shape[-1]
    tm, tn = tiles.out_proj
    h_mid = _matmul_res([(attn.reshape(m, attn_w), w_out_b, 0, attn_w),
                         (four.reshape(m, four_w), w_out_b, 1, four_w)],
                        x.reshape(m, d), tm=tm, tn=tn)
    meta_res = jnp.broadcast_to(meta_tokens[None], (b, N_META, d)).reshape(n_meta_rows, d)
    h_mid_meta = _matmul_res([(attn_meta.reshape(n_meta_rows, attn_w), w_out_b, 0, attn_w),
                              (four_meta.reshape(n_meta_rows, four_w), w_out_b, 1, four_w)],
                             meta_res, tm=n_meta_rows, tn=tn)

    tm, tc = tiles.up_glu
    halo = _gate_halo(h_mid, h_mid_meta, norm2_w[0], w_up_b, b=b, seq=seq, tm=tm, d_ff=d_ff)
    z = _up_glu(h_mid, norm2_w[0], w_up_b, halo, conv_w[0], conv_b[0], tm=tm, tc=tc)
    tm, tn = tiles.down_proj
    h_out = _matmul_res([(z, w_down_b, 0, d_ff)], h_mid, tm=tm, tn=tn)

    return _rmsnorm(h_out, final_norm_w, tm=tiles.final_norm).reshape(b, seq, d)
```

```python
import functools
import math
from typing import NamedTuple

import jax
import jax.numpy as jnp
from jax import lax
from jax.experimental import pallas as pl
from jax.experimental.pallas import tpu as pltpu

N_META = 16
N_HEADS = 16
HEAD_DIM = 64
V_DIM = 128
LANES = 128
N_GROUPS = 4
GROUP_DIM = 512
CONV_WIDTH = 3
NORM_EPS = 1e-6
SUBLN_EPS = 1e-5
LAMBDA_INIT = 0.8 - 0.6 * math.exp(-0.3 * 0)

VMEM_LIMIT_BYTES = 56 * 1024 * 1024
NORM_ROWS = 256
BF16 = jnp.bfloat16
F32 = jnp.float32


def _params(*sem):
    return pltpu.CompilerParams(dimension_semantics=sem, vmem_limit_bytes=VMEM_LIMIT_BYTES)


def _norm_rows(x_ref, nw_ref, rows):
    x = x_ref[rows, :]
    inv = lax.rsqrt(jnp.mean(x * x, axis=-1, keepdims=True) + NORM_EPS)
    return (x * inv * nw_ref[...]).astype(BF16)


def _prefetch_row_tile(x_hbm, x_ref, x_sem):
    i, j = pl.program_id(0), pl.program_id(1)
    tm = x_ref.shape[0]

    def copy(tile):
        rows = pl.ds(pl.multiple_of(tile * tm, tm), tm)
        return pltpu.make_async_copy(x_hbm.at[rows, :], x_ref, x_sem)

    @pl.when((i == 0) & (j == 0))
    def _():
        copy(i).start()

    @pl.when((j == 1) & (i + 1 < pl.num_programs(0)))
    def _():
        copy(i + 1).start()

    return copy(i).wait


def _norm_matmul_kernel(x_hbm, nw_ref, w_ref, o_ref, hn_ref, x_ref, x_sem, *, row_chunks):
    wait_rows = _prefetch_row_tile(x_hbm, x_ref, x_sem)

    @pl.when(pl.program_id(1) == 0)
    def _():
        wait_rows()
        for rows in row_chunks:
            hn = _norm_rows(x_ref, nw_ref, rows)
            hn_ref[rows, :] = hn
            o_ref[rows, :] = jnp.dot(hn, w_ref[...], preferred_element_type=F32).astype(o_ref.dtype)

    @pl.when(pl.program_id(1) != 0)
    def _():
        o_ref[...] = jnp.dot(hn_ref[...], w_ref[...], preferred_element_type=F32).astype(o_ref.dtype)


def _norm_matmul(x, nw, w, *, tm, tn, n_cols=None):
    m, d = x.shape
    n = w.shape[1] if n_cols is None else n_cols
    step = min(tm, NORM_ROWS)
    row_chunks = tuple(slice(r, r + step) for r in range(0, tm, step))
    assert n // tn >= 2, "the next row tile is prefetched at column step 1"
    return pl.pallas_call(
        functools.partial(_norm_matmul_kernel, row_chunks=row_chunks),
        grid=(m // tm, n // tn),
        in_specs=[
            pl.BlockSpec(memory_space=pl.ANY),
            pl.BlockSpec((1, d), lambda i, j: (0, 0)),
            pl.BlockSpec((d, tn), lambda i, j: (0, j)),
        ],
        out_specs=pl.BlockSpec((tm, tn), lambda i, j: (i, j)),
        out_shape=jax.ShapeDtypeStruct((m, n), BF16),
        scratch_shapes=[pltpu.VMEM((tm, d), BF16), pltpu.VMEM((tm, d), F32),
                        pltpu.SemaphoreType.DMA(())],
        compiler_params=_params("arbitrary", "arbitrary"),
        name="norm_matmul",
    )(x, nw.reshape(1, d), w)


META_PAD = 128
SUM_ROWS = 16
LOG2E = 1.0 / math.log(2.0)
BEFORE, DIAG, AFTER = 0, 1, 2
SKIP_BITS = 160.0
NORM_SLACK = 1.01


def _attn_kernel(*refs, t, n_cast):
    (lq1_ref, lk1_ref, lq2_ref, lk2_ref, subw_ref, q_ref, qm_ref, k_ref, v_ref, km_ref,
     vm_ref) = refs[:N_ATTN_INPUTS]
    cast_in = refs[N_ATTN_INPUTS:N_ATTN_INPUTS + n_cast]
    o_ref, om_ref = refs[N_ATTN_INPUTS + n_cast:N_ATTN_INPUTS + n_cast + 2]
    cast_out = refs[N_ATTN_INPUTS + n_cast + 2:N_ATTN_INPUTS + 2 * n_cast + 2]
    (vt_ref, vtm_ref, bias_ref, ta_ref, tb_ref, ma_ref, mb_ref, m_ref, acc_ref,
     knorm_ref) = refs[N_ATTN_INPUTS + 2 * n_cast + 2:]
    for w_ref, wb_ref in zip(cast_in, cast_out):
        wb_ref[...] = w_ref[...].astype(wb_ref.dtype)

    h = pl.program_id(1)
    qi = pl.program_id(2)
    seq = k_ref.shape[0]
    n_chunks = seq // t
    assert n_chunks >= 2, "every query tile needs a neighbouring key chunk"
    acc_rows = V_DIM + SUM_ROWS

    slope = LOG2E * jnp.exp2(jnp.full((1, 1), -8.0 / N_HEADS, F32) * jnp.asarray(h + 1, F32))
    lane = lax.broadcasted_iota(jnp.int32, (1, V_DIM), 1)
    map_cols = (lax.broadcasted_iota(jnp.int32, (V_DIM, V_DIM), 0) // HEAD_DIM
                == lax.broadcasted_iota(jnp.int32, (V_DIM, V_DIM), 1)).astype(BF16)
    map_rows = (lax.broadcasted_iota(jnp.int32, (8, V_DIM), 1) // HEAD_DIM
                == lax.broadcasted_iota(jnp.int32, (8, V_DIM), 0)).astype(BF16)

    @pl.when(qi == 0)
    def _():
        for c in range(n_chunks):
            vt_ref[0:V_DIM, c * t:(c + 1) * t] = v_ref[c * t:(c + 1) * t, :].astype(F32).T.astype(BF16)
        vt_ref[V_DIM:acc_rows, :] = jnp.ones((SUM_ROWS, seq), BF16)
        vtm_ref[0:V_DIM, :] = vm_ref[...].astype(F32).T.astype(BF16)
        vtm_ref[V_DIM:acc_rows, :] = jnp.ones((SUM_ROWS, META_PAD), BF16)
        row = lax.broadcasted_iota(jnp.int32, (t, t), 0)
        col = lax.broadcasted_iota(jnp.int32, (t, t), 1)
        bias_ref[BEFORE] = slope * row.astype(F32)
        bias_ref[DIAG] = -slope * jnp.abs(col - row).astype(F32)
        bias_ref[AFTER] = -slope * row.astype(F32)
        k_sq = jnp.square(k_ref[...].astype(F32)).astype(BF16)
        sq_norms = jnp.dot(k_sq, map_cols, preferred_element_type=F32)
        knorm_ref[0:1, :] = jnp.sqrt(jnp.max(sq_norms, axis=0, keepdims=True))

    lam = (jnp.exp(jnp.sum(lq1_ref[...] * lk1_ref[...], axis=-1, keepdims=True))
           - jnp.exp(jnp.sum(lq2_ref[...] * lk2_ref[...], axis=-1, keepdims=True))
           + LAMBDA_INIT)
    meta_row = lax.broadcasted_iota(jnp.int32, (N_META, 1), 0)

    def attend(q, q_pos, tile_idx, out_ref):
        nq = q.shape[0]
        q32 = q.astype(F32) * (LOG2E / math.sqrt(HEAD_DIM))
        qs = (jnp.where(lane < HEAD_DIM, q32, 0.0).astype(BF16),
              jnp.where(lane >= HEAD_DIM, q32, 0.0).astype(BF16))

        def scores(k_c):
            return tuple(lax.dot_general(k_c, qc, (((1,), (1,)), ((), ())), preferred_element_type=F32)
                         for qc in qs)

        def update(c, tt, tt_max, row_bias, vt_c):
            m = m_ref[c, 0:1, 0:nq]
            m_new = jnp.maximum(m, tt_max + row_bias)
            p = jnp.exp2(tt - (m_new - row_bias))
            acc_ref[c, :, 0:nq] = (jnp.exp2(m - m_new) * acc_ref[c, :, 0:nq]
                                   + jnp.dot(vt_c, p.astype(BF16), preferred_element_type=F32))
            m_ref[c, 0:1, 0:nq] = m_new

        def produce(c_idx, kind, buf_ref, max_ref):
            k_c = k_ref[pl.ds(pl.multiple_of(c_idx * t, t), t), :]
            key_bias = bias_ref[kind, :, 0:nq]
            for c, s in enumerate(scores(k_c)):
                tt = s + key_bias
                buf_ref[c, :, 0:nq] = tt
                max_ref[c, 0:1, 0:nq] = jnp.max(tt, axis=0, keepdims=True)

        def consume(c_idx, kind, buf_ref, max_ref):
            sign = 1.0 - kind
            k_pos0 = jnp.asarray(N_META + c_idx * t, F32)
            row_bias = (-sign) * slope * (q_pos - k_pos0)
            vt_c = vt_ref[:, pl.ds(pl.multiple_of(c_idx * t, t), t)]
            for c in range(2):
                update(c, buf_ref[c, :, 0:nq], max_ref[c, 0:1, 0:nq], row_bias, vt_c)

        m_ref[...] = jnp.full(m_ref.shape, -jnp.inf, F32)
        acc_ref[...] = jnp.zeros(acc_ref.shape, F32)
        buf_a, buf_b = (ta_ref, ma_ref), (tb_ref, mb_ref)
        if tile_idx is None:
            first = (jnp.int32(0), AFTER)
        else:
            produce(tile_idx, DIAG, *buf_a)
            has_before = tile_idx > 0
            first = (jnp.where(has_before, tile_idx - 1, tile_idx + 1),
                     jnp.where(has_before, BEFORE, AFTER))

        if tile_idx is None:
            key_bias, row_bias = -slope * jnp.abs(q_pos - meta_row.astype(F32)), jnp.zeros((1, nq), F32)
        else:
            key_bias, row_bias = bias_ref[BEFORE, 0:N_META, 0:nq], -slope * q_pos
        for c, s in enumerate(scores(km_ref[0:N_META, :])):
            tt = s + key_bias
            update(c, tt, jnp.max(tt, axis=0, keepdims=True), row_bias, vtm_ref[:, 0:N_META])

        produce(*first, *buf_b)
        if tile_idx is not None:
            consume(tile_idx, DIAG, *buf_a)

        q_sq = jnp.square(q32.astype(BF16).astype(F32)).astype(BF16)
        q_norm = jnp.sqrt(lax.dot_general(map_rows, q_sq, (((1,), (1,)), ((), ())),
                                          preferred_element_type=F32))
        bound = jnp.full((1, 1), -jnp.inf, F32)
        for c in range(2):
            gap = knorm_ref[0:1, c:c + 1] * q_norm[c:c + 1] * NORM_SLACK - m_ref[c, 0:1, 0:nq]
            bound = jnp.maximum(bound, jnp.max(gap, axis=-1, keepdims=True))
        reach = ((bound + SKIP_BITS) / slope - 1.0) / t + 1.0
        n_side = jnp.max(jnp.clip(jnp.floor(reach), 1.0, float(n_chunks))).astype(jnp.int32)
        if tile_idx is None:
            n_before, n_todo = 0, jnp.minimum(n_chunks, n_side)
        else:
            n_before = jnp.minimum(tile_idx, n_side)
            n_todo = n_before + jnp.minimum(n_chunks - 1 - tile_idx, n_side)

        def todo(k):
            if tile_idx is None:
                return k, AFTER
            before = k < n_before
            c_idx = jnp.where(before, tile_idx - 1 - k, tile_idx + 1 + k - n_before)
            return c_idx, jnp.where(before, BEFORE, AFTER)

        def pair(j, carry):
            k = 2 * j
            produce(*todo(k + 1), *buf_a)
            consume(*todo(k), *buf_b)
            produce(*todo(k + 2), *buf_b)
            consume(*todo(k + 1), *buf_a)
            return carry

        n_pairs = (n_todo - 1) // 2
        lax.fori_loop(0, n_pairs, pair, 0)

        @pl.when(n_todo - 2 * n_pairs == 2)
        def _():
            produce(*todo(n_todo - 1), *buf_a)
            consume(*todo(n_todo - 2), *buf_b)
            consume(*todo(n_todo - 1), *buf_a)

        @pl.when(n_todo - 2 * n_pairs == 1)
        def _():
            consume(*todo(n_todo - 1), *buf_b)

        acc1, acc2 = acc_ref[0, :, 0:nq], acc_ref[1, :, 0:nq]
        a = (acc1[0:V_DIM] * (1.0 / acc1[V_DIM:V_DIM + 1])
             - acc2[0:V_DIM] * (lam / acc2[V_DIM:V_DIM + 1]))
        inv = lax.rsqrt(jnp.mean(a * a, axis=0, keepdims=True) + SUBLN_EPS)
        out_ref[...] = ((a * inv).T * subw_ref[...] * (1.0 - LAMBDA_INIT)).astype(out_ref.dtype)

    for sub in range(q_ref.shape[0] // t):
        tile = qi * (q_ref.shape[0] // t) + sub
        q_pos = (N_META + tile * t + lax.broadcasted_iota(jnp.int32, (1, t), 1)).astype(F32)
        attend(q_ref[sub * t:(sub + 1) * t, :], q_pos, tile, o_ref.at[sub * t:(sub + 1) * t, :])

    @pl.when(qi == 0)
    def _():
        pos = jnp.minimum(lax.broadcasted_iota(jnp.int32, (1, META_PAD), 1), N_META - 1)
        attend(qm_ref[...], pos.astype(F32), None, om_ref)


N_ATTN_INPUTS = 11
BF16_ROWS = 16


def _slab_rows(w, n_steps):
    n_rows = w.shape[0]
    for rows in range(BF16_ROWS, n_rows + 1, BF16_ROWS):
        if n_rows % rows == 0 and n_rows // rows <= n_steps:
            return rows
    return None


def _attention(lams, subw, y, y_meta_pad, cast_weights, *, t, tiles_per_step):
    b, seq, _ = y.shape
    tq = t * tiles_per_step
    n_q = seq // tq
    h_blk = N_HEADS
    vec = lambda n: pl.BlockSpec((1, n), lambda bi, h, i: (0, 0))

    def slab(w):
        rows = _slab_rows(w, b * N_HEADS * n_q)
        last = w.shape[0] // rows - 1
        return pl.BlockSpec((rows, w.shape[1]),
                            lambda bi, h, i: (jnp.minimum((bi * N_HEADS + h) * n_q + i, last), 0))

    acc_rows = V_DIM + SUM_ROWS
    return pl.pallas_call(
        functools.partial(_attn_kernel, t=t, n_cast=len(cast_weights)),
        grid=(b, N_HEADS, n_q),
        in_specs=[
            vec(HEAD_DIM), vec(HEAD_DIM), vec(HEAD_DIM), vec(HEAD_DIM), vec(V_DIM),
            pl.BlockSpec((None, tq, V_DIM), lambda bi, h, i: (bi, i, h)),
            pl.BlockSpec((META_PAD, V_DIM), lambda bi, h, i: (0, h)),
            pl.BlockSpec((None, seq, V_DIM), lambda bi, h, i: (bi, 0, h_blk + h)),
            pl.BlockSpec((None, seq, V_DIM), lambda bi, h, i: (bi, 0, 2 * h_blk + h)),
            pl.BlockSpec((META_PAD, V_DIM), lambda bi, h, i: (0, h_blk + h)),
            pl.BlockSpec((META_PAD, V_DIM), lambda bi, h, i: (0, 2 * h_blk + h)),
        ] + [slab(w) for w in cast_weights],
        out_specs=[
            pl.BlockSpec((None, tq, V_DIM), lambda bi, h, i: (bi, i, h)),
            pl.BlockSpec((None, META_PAD, V_DIM), lambda bi, h, i: (bi, 0, h)),
        ] + [slab(w) for w in cast_weights],
        out_shape=[
            jax.ShapeDtypeStruct((b, seq, N_HEADS * V_DIM), BF16),
            jax.ShapeDtypeStruct((b, META_PAD, N_HEADS * V_DIM), BF16),
        ] + [jax.ShapeDtypeStruct(w.shape, BF16) for w in cast_weights],
        scratch_shapes=[
            pltpu.VMEM((acc_rows, seq), BF16),
            pltpu.VMEM((acc_rows, META_PAD), BF16),
            pltpu.VMEM((3, t, t), F32),
            pltpu.VMEM((2, t, t), F32),
            pltpu.VMEM((2, t, t), F32),
            pltpu.VMEM((2, 8, t), F32),
            pltpu.VMEM((2, 8, t), F32),
            pltpu.VMEM((2, 8, t), F32),
            pltpu.VMEM((2, acc_rows, t), F32),
            pltpu.VMEM((8, V_DIM), F32),
        ],
        compiler_params=_params("arbitrary", "arbitrary", "arbitrary"),
        name="diff_attention",
    )(*[v.reshape(1, HEAD_DIM) for v in lams], subw.reshape(1, V_DIM),
      y, y_meta_pad, y, y, y_meta_pad, y_meta_pad, *cast_weights)


def _chan_dft_kernel(f_ref, cs_ref, o_ref):
    t = jnp.dot(f_ref[...], cs_ref[...], preferred_element_type=F32)
    o_ref[0] = t[:, :GROUP_DIM].astype(o_ref.dtype)
    o_ref[1] = t[:, GROUP_DIM:].astype(o_ref.dtype)


def _chan_dft(y, cs, *, tr):
    b, rows, in_width = y.shape
    f_blk0 = in_width // GROUP_DIM - N_GROUPS
    return pl.pallas_call(
        _chan_dft_kernel,
        grid=(b, rows // tr, N_GROUPS),
        in_specs=[
            pl.BlockSpec((None, tr, GROUP_DIM), lambda bi, r, g: (bi, r, f_blk0 + g)),
            pl.BlockSpec((GROUP_DIM, 2 * GROUP_DIM), lambda bi, r, g: (0, 0)),
        ],
        out_specs=pl.BlockSpec((None, 2, tr, GROUP_DIM), lambda bi, r, g: (bi, 0, r, g)),
        out_shape=jax.ShapeDtypeStruct((b, 2, rows, N_GROUPS * GROUP_DIM), BF16),
        compiler_params=_params("parallel", "parallel", "parallel"),
        name="chan_dft",
    )(y, cs)


def _pos_dft_kernel(mr_ref, mm_ref, t_ref, tm_ref, nw_ref, o_ref, *, scale):
    r = jnp.dot(mr_ref[...], t_ref[...], preferred_element_type=F32)
    r = r + jnp.dot(mm_ref[...], tm_ref[...], preferred_element_type=F32)
    r = r * scale
    inv = lax.rsqrt(jnp.mean(r * r, axis=-1, keepdims=True) + NORM_EPS)
    o_ref[...] = (r * inv * nw_ref[...]).astype(o_ref.dtype)


def _pos_dft(mr, mm, t, t_meta, nw, *, tp, scale):
    p_rows = mr.shape[0]
    b, k_real, width = t.shape
    k_meta = t_meta.shape[1]
    return pl.pallas_call(
        functools.partial(_pos_dft_kernel, scale=scale),
        grid=(p_rows // tp, b, N_GROUPS),
        in_specs=[
            pl.BlockSpec((tp, k_real), lambda p, bi, g: (p, 0)),
            pl.BlockSpec((tp, k_meta), lambda p, bi, g: (p, 0)),
            pl.BlockSpec((None, k_real, GROUP_DIM), lambda p, bi, g: (bi, 0, g)),
            pl.BlockSpec((None, k_meta, GROUP_DIM), lambda p, bi, g: (0, 0, g)),
            pl.BlockSpec((1, GROUP_DIM), lambda p, bi, g: (0, g)),
        ],
        out_specs=pl.BlockSpec((None, tp, GROUP_DIM), lambda p, bi, g: (bi, p, g)),
        out_shape=jax.ShapeDtypeStruct((b, p_rows, width), BF16),
        compiler_params=_params("parallel", "parallel", "arbitrary"),
        name="pos_dft",
    )(mr, mm, t, t_meta, nw.reshape(1, width))


TABLE_STEP = 64


def _dft_tables(seq):
    l_total = N_META + seq
    c = jnp.arange(GROUP_DIM, dtype=jnp.int32)
    ang_c = ((c[:, None] * c[None, :]) % GROUP_DIM).astype(F32) * (2.0 * math.pi / GROUP_DIM)
    cs = jnp.concatenate([jnp.cos(ang_c), jnp.sin(ang_c)], axis=1).astype(BF16)

    to_rad = 2.0 * math.pi / l_total
    half_pi = jnp.asarray([0.0, 0.5 * math.pi], F32)
    meta_p = jnp.arange(N_META, dtype=jnp.int32)
    real_p = N_META + jnp.arange(seq, dtype=jnp.int32)
    real_l, real_phase = jnp.tile(real_p, 2), jnp.repeat(half_pi, seq)
    meta_l, meta_phase = jnp.tile(meta_p, 2), jnp.repeat(half_pi, N_META)

    def angle(p, l, phase):
        return ((p[:, None] * l[None, :]) % l_total).astype(F32) * to_rad + phase[None, :]

    def direct(p, l, phase):
        return jnp.cos(angle(p, l, phase)).astype(BF16)

    hi = angle(N_META + TABLE_STEP * jnp.arange(seq // TABLE_STEP, dtype=jnp.int32), real_l, 0.0 * real_phase)
    lo = angle(jnp.arange(TABLE_STEP, dtype=jnp.int32), real_l, real_phase)
    real_rows_real_cols = (jnp.cos(hi)[:, None, :] * jnp.cos(lo)[None, :, :]
                           - jnp.sin(hi)[:, None, :] * jnp.sin(lo)[None, :, :]
                           ).reshape(seq, 2 * seq).astype(BF16)
    real_rows = (real_rows_real_cols, direct(real_p, meta_l, meta_phase))
    meta_rows = (direct(meta_p, real_l, real_phase), direct(meta_p, meta_l, meta_phase))
    return cs, real_rows, meta_rows


def _matmul_res_kernel(*refs, n_pairs):
    res_ref, o_ref = refs[2 * n_pairs], refs[2 * n_pairs + 1]
    acc = res_ref[...]
    for k in range(n_pairs):
        acc = acc + jnp.dot(refs[2 * k][...], refs[2 * k + 1][...], preferred_element_type=F32)
    o_ref[...] = acc


def _matmul_res(pairs, res, *, tm, tn):
    m, n = res.shape
    in_specs, args = [], []
    for a, w, blk, k in pairs:
        in_specs += [pl.BlockSpec((tm, k), lambda i, j: (i, 0)),
                     pl.BlockSpec((k, tn), lambda i, j, blk=blk: (blk, j))]
        args += [a, w]
    in_specs.append(pl.BlockSpec((tm, tn), lambda i, j: (i, j)))
    return pl.pallas_call(
        functools.partial(_matmul_res_kernel, n_pairs=len(pairs)),
        grid=(m // tm, n // tn),
        in_specs=in_specs,
        out_specs=pl.BlockSpec((tm, tn), lambda i, j: (i, j)),
        out_shape=jax.ShapeDtypeStruct((m, n), F32),
        compiler_params=_params("parallel", "arbitrary"),
        name="matmul_res",
    )(*args, res)


HALO_ROWS = 8


def _up_glu_kernel(x_hbm, nw_ref, wg_ref, wv_ref, halo_ref, cw_ref, cb_ref, o_ref, hn_ref,
                   x_ref, x_sem, *, tm, tail):
    c = pl.program_id(1)
    wait_rows = _prefetch_row_tile(x_hbm, x_ref, x_sem)

    @pl.when(c == 0)
    def _():
        wait_rows()

        def norm_rows(r, carry):
            rows = pl.ds(pl.multiple_of(r * NORM_ROWS, NORM_ROWS), NORM_ROWS)
            hn_ref[rows, :] = _norm_rows(x_ref, nw_ref, rows)
            return carry
        lax.fori_loop(0, tm // NORM_ROWS, norm_rows, 0)

    hn = hn_ref[...]
    g = jnp.dot(hn, wg_ref[...], preferred_element_type=F32)
    v = jnp.dot(hn, wv_ref[...], preferred_element_type=F32)
    row = lax.broadcasted_iota(jnp.int32, g.shape, 0)
    g_prev = jnp.where(row == 0, halo_ref[0, 0:1, :], pltpu.roll(g, 1, axis=0))
    g_next = jnp.where(row == tm - 1, halo_ref[0, 1:2, :], pltpu.roll(g, tm - 1, axis=0))
    conv = g_prev * cw_ref[0:1, :] + g * cw_ref[1:2, :] + g_next * cw_ref[2:3, :] + cb_ref[...]
    gelu = 0.5 * conv * (1.0 + lax.erf(conv * math.sqrt(0.5)))
    z = (gelu * v).astype(o_ref.dtype)
    tc = z.shape[1]
    if tail == tc:
        o_ref[...] = z
    else:
        @pl.when(c < pl.num_programs(1) - 1)
        def _():
            o_ref[...] = z

        @pl.when(c == pl.num_programs(1) - 1)
        def _():
            o_ref[:, 0:tail] = z[:, tc - tail:]
            o_ref[:, tail:] = jnp.zeros((tm, tc - tail), o_ref.dtype)


def _up_glu(x, nw, w_up, halo, conv_w, conv_b, *, tm, tc):
    m, d = x.shape
    d_ff = w_up.shape[1] // 2
    assert d_ff % LANES == 0 and d_ff >= tc
    n_c = pl.cdiv(d_ff, tc)
    assert n_c >= 2, "the next row tile is prefetched at column step 1"
    tail = d_ff - (n_c - 1) * tc
    el = pl.Element

    def col(c, base=0):
        return pl.multiple_of(base + jnp.minimum(c * tc, d_ff - tc), LANES)

    return pl.pallas_call(
        functools.partial(_up_glu_kernel, tm=tm, tail=tail),
        grid=(m // tm, n_c),
        in_specs=[
            pl.BlockSpec(memory_space=pl.ANY),
            pl.BlockSpec((1, d), lambda i, c: (0, 0)),
            pl.BlockSpec((el(d), el(tc)), lambda i, c: (0, col(c))),
            pl.BlockSpec((el(d), el(tc)), lambda i, c: (0, col(c, d_ff))),
            pl.BlockSpec((el(1), el(HALO_ROWS), el(tc)), lambda i, c: (i, 0, col(c))),
            pl.BlockSpec((el(CONV_WIDTH), el(tc)), lambda i, c: (0, col(c))),
            pl.BlockSpec((el(1), el(tc)), lambda i, c: (0, col(c))),
        ],
        out_specs=pl.BlockSpec((tm, tc), lambda i, c: (i, c)),
        out_shape=jax.ShapeDtypeStruct((m, n_c * tc), BF16),
        scratch_shapes=[pltpu.VMEM((tm, d), BF16), pltpu.VMEM((tm, d), F32),
                        pltpu.SemaphoreType.DMA(())],
        compiler_params=_params("arbitrary", "arbitrary"),
        name="up_glu",
    )(x, nw.reshape(1, d), w_up, w_up, halo, conv_w, conv_b.reshape(1, d_ff))


def _gate_halo(h_mid, h_mid_meta, nw, w_up, *, b, seq, tm, d_ff):
    d = h_mid.shape[-1]
    n_t = seq // tm
    h3 = h_mid.reshape(b, seq, d)
    pairs = [h3[:, k * tm - 1:k * tm + 1] for k in range(1, n_t)]
    meta_last = h_mid_meta.reshape(b, N_META, d)[:, N_META - 1:]
    rows = jnp.concatenate([meta_last] + pairs, axis=1).reshape(-1, d)
    n_rows = rows.shape[0]
    pad = -n_rows % 16
    rows = jnp.pad(rows, ((0, pad), (0, 0)))
    gates = _norm_matmul(rows, nw, w_up, tm=n_rows + pad, tn=256, n_cols=d_ff)[:n_rows]
    gates = gates.astype(F32).reshape(b, 2 * n_t - 1, d_ff)
    before = jnp.concatenate([gates[:, 0:1], gates[:, 1::2]], axis=1)
    after = jnp.concatenate([gates[:, 2::2], jnp.zeros((b, 1, d_ff), F32)], axis=1)
    halo = jnp.stack([before, after], axis=2)
    halo = jnp.pad(halo, ((0, 0), (0, 0), (0, HALO_ROWS - 2), (0, 0)))
    return halo.reshape(b * n_t, HALO_ROWS, d_ff)


def _rmsnorm_kernel(x_ref, w_ref, o_ref):
    x = x_ref[...]
    inv = lax.rsqrt(jnp.mean(x * x, axis=-1, keepdims=True) + NORM_EPS)
    o_ref[...] = x * inv * w_ref[...]


def _rmsnorm(x, w, *, tm):
    m, d = x.shape
    return pl.pallas_call(
        _rmsnorm_kernel,
        grid=(m // tm,),
        in_specs=[pl.BlockSpec((tm, d), lambda i: (i, 0)), pl.BlockSpec((1, d), lambda i: (0, 0))],
        out_specs=pl.BlockSpec((tm, d), lambda i: (i, 0)),
        out_shape=jax.ShapeDtypeStruct((m, d), F32),
        compiler_params=_params("parallel"),
        name="final_rmsnorm",
    )(x, w.reshape(1, d))


class _Tiles(NamedTuple):
    in_proj: tuple
    attention: tuple
    chan_dft: int
    pos_dft: int
    out_proj: tuple
    up_glu: tuple
    down_proj: tuple
    final_norm: int


def _tiles(seq, d):
    t_attn = min(512, seq)
    return _Tiles(in_proj=(1024, 1024), attention=(t_attn, 8 if seq % (8 * t_attn) == 0 else 1),
                  chan_dft=min(1024, seq),
                  pos_dft=min(1024, seq), out_proj=(1024, min(1024, d)),
                  up_glu=(min(1024, seq), 512), down_proj=(512, min(512, d)), final_norm=512)


def kernel(x, meta_tokens, norm1_w, w_in, lambda_q1, lambda_k1, lambda_q2, lambda_k2,
           diff_subln_w, fourier_norm_w, w_out, norm2_w, w_up, conv_w, conv_b, w_down,
           final_norm_w):
    b, seq, d = x.shape
    m, n_meta_rows = b * seq, b * N_META
    in_width = w_in.shape[-1]
    d_ff = w_down.shape[1]
    assert w_in.shape[0] == 1 and meta_tokens.shape[0] == N_META
    tiles = _tiles(seq, d)

    w_in_b = w_in[0].astype(BF16)
    lams = (lambda_q1[0], lambda_k1[0], lambda_q2[0], lambda_k2[0])

    later = (w_out[0], w_up[0], w_down[0])
    t_attn, tiles_per_step = tiles.attention
    attn_steps = b * N_HEADS * (seq // (t_attn * tiles_per_step))
    in_attention = all(_slab_rows(w, attn_steps) for w in later)

    tm, tn = tiles.in_proj
    y = _norm_matmul(x.reshape(m, d), norm1_w[0], w_in_b, tm=tm, tn=tn).reshape(b, seq, in_width)
    y_meta = _norm_matmul(meta_tokens, norm1_w[0], w_in_b, tm=N_META, tn=tn)

    y_meta_pad = jnp.pad(y_meta, ((0, META_PAD - N_META), (0, 0)))
    attn, attn_meta, *cast = _attention(lams, diff_subln_w[0], y, y_meta_pad,
                                        later if in_attention else (), t=t_attn,
                                        tiles_per_step=tiles_per_step)
    attn_meta = attn_meta[:, :N_META]
    w_out_b, w_up_b, w_down_b = cast if in_attention else (w.astype(BF16) for w in later)

    cs, real_rows, meta_rows = _dft_tables(seq)
    four_w = N_GROUPS * GROUP_DIM
    t = _chan_dft(y, cs, tr=tiles.chan_dft).reshape(b, 2 * seq, four_w)
    t_meta = _chan_dft(y_meta[None], cs, tr=N_META).reshape(1, 2 * N_META, four_w)
    scale = 1.0 / math.sqrt((N_META + seq) * GROUP_DIM)
    four = _pos_dft(*real_rows, t, t_meta, fourier_norm_w[0], tp=tiles.pos_dft, scale=scale)
    four_meta = _pos_dft(*meta_rows, t, t_meta, fourier_norm_w[0], tp=N_META, scale=scale)

    attn_w = attn.shape[-1]
    tm, tn = tiles.out_proj
    h_mid = _matmul_res([(attn.reshape(m, attn_w), w_out_b, 0, attn_w),
                         (four.reshape(m, four_w), w_out_b, 1, four_w)],
                        x.reshape(m, d), tm=tm, tn=tn)
    meta_res = jnp.broadcast_to(meta_tokens[None], (b, N_META, d)).reshape(n_meta_rows, d)
    h_mid_meta = _matmul_res([(attn_meta.reshape(n_meta_rows, attn_w), w_out_b, 0, attn_w),
                              (four_meta.reshape(n_meta_rows, four_w), w_out_b, 1, four_w)],
                             meta_res, tm=n_meta_rows, tn=tn)

    tm, tc = tiles.up_glu
    halo = _gate_halo(h_mid, h_mid_meta, norm2_w[0], w_up_b, b=b, seq=seq, tm=tm, d_ff=d_ff)
    z = _up_glu(h_mid, norm2_w[0], w_up_b, halo, conv_w[0], conv_b[0], tm=tm, tc=tc)
    tm, tn = tiles.down_proj
    h_out = _matmul_res([(z, w_down_b, 0, d_ff)], h_mid, tm=tm, tn=tn)

    return _rmsnorm(h_out, final_norm_w, tm=tiles.final_norm).reshape(b, seq, d)
```
